```python
import math
import jax, jax.numpy as jnp
from jax import lax
import numpy as np

D_MODEL = 1024
BATCH = 16
SEQ = 4096
DEPTH = 1
DEC_BATCH = 4
DEC_SEQ = 4096
PAST_LEN = 128

HEAD_DIM = 64
N_HEADS = 8
N_KV_HEADS = 2
GROUP = N_HEADS // N_KV_HEADS
WINDOW = 128
BLOCK = 128
NUM_BUCKETS = 32
MAX_DISTANCE = 128
F_GROUPS = 4
F_CH = 128
ATTN_W = N_HEADS * HEAD_DIM
FOUR_W = F_GROUPS * F_CH
MIX_W = ATTN_W + FOUR_W
Q_COLS = ATTN_W
KV_COLS = N_KV_HEADS * HEAD_DIM
IN_COLS = Q_COLS + 2 * KV_COLS + FOUR_W
D_FF = 2816
CONV_W = 3
EPS = 1e-5
ALPHA = (2.0 * DEPTH) ** 0.25
BETA = (8.0 * DEPTH) ** -0.25

kernel_name = "hybrid_swa_fnet_convffn_encoder"


def layer_norm(x, g, b):
    xf = x.astype(jnp.float32)
    mu = jnp.mean(xf, axis=-1, keepdims=True)
    var = jnp.mean(jnp.square(xf - mu), axis=-1, keepdims=True)
    y = (xf - mu) * lax.rsqrt(var + EPS) * g.astype(jnp.float32) + b.astype(jnp.float32)
    return y.astype(x.dtype)


def rms_norm(x, g):
    xf = x.astype(jnp.float32)
    y = xf * lax.rsqrt(jnp.mean(jnp.square(xf), axis=-1, keepdims=True) + EPS) * g.astype(jnp.float32)
    return y.astype(x.dtype)


def t5_bucket(rel):
    nb = NUM_BUCKETS // 2
    ret = jnp.where(rel > 0, nb, 0)
    n = jnp.abs(rel)
    max_exact = nb // 2
    nf = jnp.maximum(n, 1).astype(jnp.float32)
    large = max_exact + (jnp.log(nf / max_exact) / math.log(MAX_DISTANCE / max_exact)
                         * (nb - max_exact)).astype(jnp.int32)
    large = jnp.minimum(large, nb - 1)
    return ret + jnp.where(n < max_exact, n, large)


def windowed_gqa(q, k, v, rel_table, sink):
    B, S, H, hd = q.shape
    nblk = S // BLOCK
    nkeys = 3 * BLOCK
    pad = ((0, 0), (BLOCK, BLOCK), (0, 0), (0, 0))
    kp = jnp.pad(k, pad)
    vp = jnp.pad(v, pad)
    qb = q.reshape(B, nblk, BLOCK, N_KV_HEADS, GROUP, hd).transpose(1, 0, 2, 3, 4, 5)
    band_rel = (jnp.arange(nkeys, dtype=jnp.int32)[None, :] - BLOCK) - jnp.arange(BLOCK, dtype=jnp.int32)[:, None]
    in_window = jnp.abs(band_rel) <= WINDOW
    bias = rel_table[t5_bucket(band_rel)].astype(jnp.float32)
    bias = bias.transpose(2, 0, 1).reshape(N_KV_HEADS, GROUP, BLOCK, nkeys)
    sink_f = sink.astype(jnp.float32).reshape(N_KV_HEADS, GROUP, 1)
    scale = HEAD_DIM ** -0.5

    def one_block(args):
        q_blk, i = args
        start = i * BLOCK
        k_blk = lax.dynamic_slice_in_dim(kp, start, nkeys, axis=1)
        v_blk = lax.dynamic_slice_in_dim(vp, start, nkeys, axis=1)
        kpos = start - BLOCK + jnp.arange(nkeys, dtype=jnp.int32)
        valid = in_window & ((kpos >= 0) & (kpos < S))[None, :]
        logits = jnp.einsum('bqkgd,bjkd->bkgqj', q_blk, k_blk,
                            preferred_element_type=jnp.float32) * scale + bias
        logits = jnp.where(valid, logits, -jnp.inf)
        m = jnp.maximum(jnp.max(logits, axis=-1), sink_f)
        p = jnp.exp(logits - m[..., None])
        denom = jnp.sum(p, axis=-1) + jnp.exp(sink_f - m)
        probs = p / denom[..., None]
        out = jnp.einsum('bkgqj,bjkd->bqkgd', probs, v_blk.astype(jnp.float32))
        return out.astype(q.dtype)

    out = lax.map(one_block, (qb, jnp.arange(nblk, dtype=jnp.int32)))
    return out.transpose(1, 0, 2, 3, 4, 5).reshape(B, S, H * hd)


def fourier_mix(u, w_f):
    B, S, _ = u.shape
    ug = u.reshape(B, S, F_GROUPS, F_CH).astype(jnp.float32)
    z = jnp.fft.fftn(ug, axes=(1, 3), norm="ortho").real
    y = jnp.einsum('bsgc,gcd->bsgd', z, w_f.astype(jnp.float32))
    return y.reshape(B, S, FOUR_W).astype(u.dtype)


def conv_ffn(x, w_up, conv_w, conv_b, w_down):
    h = x @ w_up
    C = h.shape[-1]
    h = lax.conv_general_dilated(h, conv_w.reshape(CONV_W, 1, C).astype(h.dtype),
                                 window_strides=(1,), padding='SAME',
                                 dimension_numbers=('NWC', 'WIO', 'NWC'),
                                 feature_group_count=C) + conv_b
    a, u = jnp.split(h, 2, axis=-1)
    return (jax.nn.gelu(a, approximate=False) * u) @ w_down


def encoder_layer(x, rel_table, w_in, sink, w_f, g_attn, g_four, w_o,
                  ln1_g, ln1_b, w_up, conv_w, conv_b, w_down, ln2_g, ln2_b):
    B, S, _ = x.shape
    h = x @ w_in
    q, k, v, f = jnp.split(h, [Q_COLS, Q_COLS + KV_COLS, Q_COLS + 2 * KV_COLS], axis=-1)
    q = q.reshape(B, S, N_HEADS, HEAD_DIM)
    k = k.reshape(B, S, N_KV_HEADS, HEAD_DIM)
    v = v.reshape(B, S, N_KV_HEADS, HEAD_DIM)
    attn = windowed_gqa(q, k, v, rel_table, sink)
    four = fourier_mix(f, w_f)
    mix = jnp.concatenate([rms_norm(attn, g_attn), rms_norm(four, g_four)], axis=-1) @ w_o
    x = layer_norm(ALPHA * x + mix, ln1_g, ln1_b)
    x = layer_norm(ALPHA * x + conv_ffn(x, w_up, conv_w, conv_b, w_down), ln2_g, ln2_b)
    return x


def encode(x, ln_in_g, ln_in_b, rel_table, w_in, attn_sink, w_fourier, g_attn, g_fourier,
           w_o, ln1_g, ln1_b, w_up, conv_w, conv_b, w_down, ln2_g, ln2_b):
    x = layer_norm(x, ln_in_g, ln_in_b)
    for l in range(DEPTH):
        x = encoder_layer(x, rel_table, w_in[l], attn_sink[l], w_fourier[l], g_attn[l], g_fourier[l],
                          w_o[l], ln1_g[l], ln1_b[l], w_up[l], conv_w[l], conv_b[l], w_down[l],
                          ln2_g[l], ln2_b[l])
    return x


def setup_inputs(seed: int = 0) -> dict:
    key = jax.random.key(seed)
    ks = jax.random.split(key, 24)
    f32 = jnp.float32
    nrm = lambda k, shape, s: jax.random.normal(k, shape, f32) * s
    w_qk = nrm(ks[2], (DEPTH, D_MODEL, Q_COLS + KV_COLS), D_MODEL ** -0.5)
    w_v = nrm(ks[3], (DEPTH, D_MODEL, KV_COLS), BETA * D_MODEL ** -0.5)
    w_fin = nrm(ks[4], (DEPTH, D_MODEL, FOUR_W), D_MODEL ** -0.5)
    return {
        "x_prompt": nrm(ks[0], (BATCH, SEQ, D_MODEL), 1.0),
        "x_sample": nrm(ks[1], (DEC_BATCH, DEC_SEQ, D_MODEL), 1.0),
        "ln_in_g": 1.0 + nrm(ks[5], (D_MODEL,), 0.05),
        "ln_in_b": nrm(ks[6], (D_MODEL,), 0.02),
        "rel_table": nrm(ks[7], (NUM_BUCKETS, N_HEADS), 0.5),
        "w_in": jnp.concatenate([w_qk, w_v, w_fin], axis=-1),
        "attn_sink": nrm(ks[8], (DEPTH, N_HEADS), 0.5),
        "w_fourier": nrm(ks[9], (DEPTH, F_GROUPS, F_CH, F_CH), BETA * F_CH ** -0.5),
        "g_attn": 1.0 + nrm(ks[10], (DEPTH, ATTN_W), 0.05),
        "g_fourier": 1.0 + nrm(ks[11], (DEPTH, FOUR_W), 0.05),
        "w_o": nrm(ks[12], (DEPTH, MIX_W, D_MODEL), BETA * MIX_W ** -0.5),
        "ln1_g": 1.0 + nrm(ks[13], (DEPTH, D_MODEL), 0.05),
        "ln1_b": nrm(ks[14], (DEPTH, D_MODEL), 0.02),
        "w_up": nrm(ks[15], (DEPTH, D_MODEL, 2 * D_FF), D_MODEL ** -0.5),
        "conv_w": nrm(ks[16], (DEPTH, CONV_W, 2 * D_FF), CONV_W ** -0.5),
        "conv_b": nrm(ks[17], (DEPTH, 2 * D_FF), 0.02),
        "w_down": nrm(ks[18], (DEPTH, D_FF, D_MODEL), BETA * D_FF ** -0.5),
        "ln2_g": 1.0 + nrm(ks[19], (DEPTH, D_MODEL), 0.05),
        "ln2_b": nrm(ks[20], (DEPTH, D_MODEL), 0.02),
    }


def reference(x_prompt, x_sample, ln_in_g, ln_in_b, rel_table, w_in, attn_sink, w_fourier,
              g_attn, g_fourier, w_o, ln1_g, ln1_b, w_up, conv_w, conv_b, w_down, ln2_g, ln2_b):
    y_prompt = encode(x_prompt, ln_in_g, ln_in_b, rel_table, w_in, attn_sink, w_fourier, g_attn,
                      g_fourier, w_o, ln1_g, ln1_b, w_up, conv_w, conv_b, w_down, ln2_g, ln2_b)
    y_sample = encode(x_sample, ln_in_g, ln_in_b, rel_table, w_in, attn_sink, w_fourier, g_attn,
                      g_fourier, w_o, ln1_g, ln1_b, w_up, conv_w, conv_b, w_down, ln2_g, ln2_b)
    return (y_prompt, y_sample)
```

```python
import functools
import math

import numpy as np
import jax
import jax.numpy as jnp
from jax import lax
from jax.experimental import pallas as pl
from jax.experimental.pallas import tpu as pltpu

D_MODEL = 1024
HEAD_DIM = 64
N_HEADS = 8
N_KV_HEADS = 2
GROUP = N_HEADS // N_KV_HEADS
WINDOW = 128
BLOCK = 128
NUM_BUCKETS = 32
MAX_DISTANCE = 128
F_GROUPS = 4
F_CH = 128
ATTN_W = N_HEADS * HEAD_DIM
FOUR_W = F_GROUPS * F_CH
KV_COLS = N_KV_HEADS * HEAD_DIM
QKV_COLS = ATTN_W + 2 * KV_COLS
IN_COLS = QKV_COLS + FOUR_W
D_FF = 2816
EPS = 1e-5
DEPTH = 1
ALPHA = (2.0 * DEPTH) ** 0.25
NEG = -1e30

ROW_TILE = 512
FF_CHUNK = 256
N_FF_CHUNKS = D_FF // FF_CHUNK
HALO = 16
DFT_ROW_TILE = 512
VMEM_LIMIT = 56 * 1024 * 1024

bf16 = jnp.bfloat16
f32 = jnp.float32


def _layer_norm(x, g, b):
    mu = jnp.mean(x, axis=-1, keepdims=True)
    xc = x - mu
    var = jnp.mean(xc * xc, axis=-1, keepdims=True)
    return xc * lax.rsqrt(var + EPS) * g + b


def _rms_norm(x, g):
    return x * lax.rsqrt(jnp.mean(x * x, axis=-1, keepdims=True) + EPS) * g


def _dot(a, b):
    return jnp.dot(a, b, preferred_element_type=f32)


def _in_proj_kernel(x_ref, g_ref, b_ref, w_ref, qkv_ref, f_ref):
    y = _layer_norm(x_ref[...], g_ref[...], b_ref[...])
    h = _dot(y.astype(bf16), w_ref[...])
    qkv_ref[...] = h[:, :QKV_COLS].astype(bf16)
    f_ref[...] = h[:, QKV_COLS:].astype(bf16)


def _in_proj(x2d, ln_g, ln_b, w_in):
    T = x2d.shape[0]
    return pl.pallas_call(
        _in_proj_kernel,
        grid=(T // ROW_TILE,),
        in_specs=[
            pl.BlockSpec((ROW_TILE, D_MODEL), lambda t: (t, 0)),
            pl.BlockSpec((1, D_MODEL), lambda t: (0, 0)),
            pl.BlockSpec((1, D_MODEL), lambda t: (0, 0)),
            pl.BlockSpec((D_MODEL, IN_COLS), lambda t: (0, 0)),
        ],
        out_specs=[
            pl.BlockSpec((ROW_TILE, QKV_COLS), lambda t: (t, 0)),
            pl.BlockSpec((ROW_TILE, FOUR_W), lambda t: (t, 0)),
        ],
        out_shape=[
            jax.ShapeDtypeStruct((T, QKV_COLS), bf16),
            jax.ShapeDtypeStruct((T, FOUR_W), bf16),
        ],
        compiler_params=pltpu.CompilerParams(
            dimension_semantics=("arbitrary",), vmem_limit_bytes=VMEM_LIMIT),
        name="in_proj",
    )(x2d, ln_g, ln_b, w_in)


def _attn_kernel(q_ref, k_ref, v_ref, bias_ref, sink_ref, o_ref, *, nblk):
    i = pl.program_id(1)
    seq = nblk * BLOCK
    starts = [
        pl.multiple_of(jnp.clip((i + c) * BLOCK, 0, seq - BLOCK), BLOCK) for c in (-1, 0, 1)
    ]
    col = lax.broadcasted_iota(jnp.int32, (BLOCK, 3 * BLOCK), 1)
    lo = jnp.where(i == 0, BLOCK, 0)
    hi = jnp.where(i == nblk - 1, 2 * BLOCK, 3 * BLOCK)
    in_seq = (col >= lo) & (col < hi)

    k_win = jnp.concatenate([k_ref[0, pl.ds(s, BLOCK), :] for s in starts], axis=0)
    v_win = jnp.concatenate([v_ref[0, pl.ds(s, BLOCK), :] for s in starts], axis=0)
    q = q_ref[0]
    outs = []
    for h in range(N_HEADS):
        g = h // GROUP
        q_h = q[:, h * HEAD_DIM:(h + 1) * HEAD_DIM]
        k_g = k_win[:, g * HEAD_DIM:(g + 1) * HEAD_DIM]
        v_g = v_win[:, g * HEAD_DIM:(g + 1) * HEAD_DIM]
        logits = lax.dot_general(q_h, k_g, (((1,), (1,)), ((), ())),
                                 preferred_element_type=f32)
        logits = jnp.where(in_seq, logits + bias_ref[h], NEG)
        sink = sink_ref[h]
        m = jnp.maximum(jnp.max(logits, axis=-1, keepdims=True), sink)
        p = jnp.exp(logits - m)
        denom = jnp.sum(p, axis=-1, keepdims=True) + jnp.exp(sink - m)
        pv = _dot(p.astype(bf16), v_g)
        outs.append(pv / denom)
    o_ref[0] = jnp.concatenate(outs, axis=-1).astype(bf16)


def _attention(qkv, bias, sink):
    B, S, _ = qkv.shape
    nblk = S // BLOCK
    k_col = ATTN_W // KV_COLS
    return pl.pallas_call(
        functools.partial(_attn_kernel, nblk=nblk),
        grid=(B, nblk),
        in_specs=[
            pl.BlockSpec((1, BLOCK, ATTN_W), lambda b, i: (b, i, 0)),
            pl.BlockSpec((1, S, KV_COLS), lambda b, i: (b, 0, k_col)),
            pl.BlockSpec((1, S, KV_COLS), lambda b, i: (b, 0, k_col + 1)),
            pl.BlockSpec((N_HEADS, BLOCK, 3 * BLOCK), lambda b, i: (0, 0, 0)),
            pl.BlockSpec(memory_space=pltpu.SMEM),
        ],
        out_specs=pl.BlockSpec((1, BLOCK, ATTN_W), lambda b, i: (b, i, 0)),
        out_shape=jax.ShapeDtypeStruct((B, S, ATTN_W), bf16),
        compiler_params=pltpu.CompilerParams(
            dimension_semantics=("arbitrary", "arbitrary"), vmem_limit_bytes=VMEM_LIMIT),
        name="attn",
    )(qkv, qkv, qkv, bias, sink)


def _seq_dft_kernel(c_ref, s_ref, f_ref, z_ref):
    f = f_ref[0]
    a = _dot(c_ref[...], f)
    p = _dot(s_ref[...], f)
    for g in range(F_GROUPS):
        lo = g * F_CH
        z_ref[0, :, 2 * lo:2 * lo + F_CH] = a[:, lo:lo + F_CH].astype(bf16)
        z_ref[0, :, 2 * lo + F_CH:2 * lo + 2 * F_CH] = p[:, lo:lo + F_CH].astype(bf16)


def _seq_dft(f3d, cs, sn):
    B, S, _ = f3d.shape
    return pl.pallas_call(
        _seq_dft_kernel,
        grid=(S // DFT_ROW_TILE, B),
        in_specs=[
            pl.BlockSpec((DFT_ROW_TILE, S), lambda r, b: (r, 0)),
            pl.BlockSpec((DFT_ROW_TILE, S), lambda r, b: (r, 0)),
            pl.BlockSpec((1, S, FOUR_W), lambda r, b: (b, 0, 0)),
        ],
        out_specs=pl.BlockSpec((1, DFT_ROW_TILE, 2 * FOUR_W), lambda r, b: (b, r, 0)),
        out_shape=jax.ShapeDtypeStruct((B, S, 2 * FOUR_W), bf16),
        compiler_params=pltpu.CompilerParams(
            dimension_semantics=("arbitrary", "arbitrary"), vmem_limit_bytes=VMEM_LIMIT),
        name="seq_dft",
    )(cs, sn, f3d)


def _mix_kernel(x_ref, attn_ref, z_ref, ccsc_ref, wf_ref, ga_ref, gf_ref, wo_ref,
                lg_ref, lb_ref, l1g_ref, l1b_ref, x1_ref):
    x0 = _layer_norm(x_ref[...], lg_ref[...], lb_ref[...])
    an = _rms_norm(attn_ref[...].astype(f32), ga_ref[...]).astype(bf16)
    z = z_ref[...]
    four = []
    for g in range(F_GROUPS):
        zc = _dot(z[:, 2 * g * F_CH:2 * (g + 1) * F_CH], ccsc_ref[...])
        four.append(_dot(zc.astype(bf16), wf_ref[g]))
    fn = _rms_norm(jnp.concatenate(four, axis=-1), gf_ref[...]).astype(bf16)
    mix = _dot(an, wo_ref[:ATTN_W, :]) + _dot(fn, wo_ref[ATTN_W:, :])
    x1_ref[...] = _layer_norm(ALPHA * x0 + mix, l1g_ref[...], l1b_ref[...])


def _mix(x2d, attn2d, z2d, ccsc, w_f, g_attn, g_four, w_o, ln_g, ln_b, ln1_g, ln1_b):
    T = x2d.shape[0]
    row = lambda w: pl.BlockSpec((ROW_TILE, w), lambda t: (t, 0))
    full = lambda *shape: pl.BlockSpec(shape, lambda t: (0,) * len(shape))
    return pl.pallas_call(
        _mix_kernel,
        grid=(T // ROW_TILE,),
        in_specs=[
            row(D_MODEL), row(ATTN_W), row(2 * FOUR_W),
            full(2 * F_CH, F_CH), full(F_GROUPS, F_CH, F_CH),
            full(1, ATTN_W), full(1, FOUR_W), full(D_MODEL, D_MODEL),
            full(1, D_MODEL), full(1, D_MODEL), full(1, D_MODEL), full(1, D_MODEL),
        ],
        out_specs=row(D_MODEL),
        out_shape=jax.ShapeDtypeStruct((T, D_MODEL), f32),
        compiler_params=pltpu.CompilerParams(
            dimension_semantics=("arbitrary",), vmem_limit_bytes=VMEM_LIMIT),
        name="mix",
    )(x2d, attn2d, z2d, ccsc, w_f, g_attn, g_four, w_o, ln_g, ln_b, ln1_g, ln1_b)


def _gelu(x):
    return 0.5 * x * (1.0 + lax.erf(x * (1.0 / math.sqrt(2.0))))


def _ffn_kernel(xp_ref, xm_ref, xn_ref, wa_ref, wu_ref, cwa_ref, cwu_ref, cba_ref, cbu_ref,
                wd_ref, l2g_ref, l2b_ref, y_ref, xs_ref, ha_ref, hu_ref, acc_ref, *, tiles_per_seq):
    t = pl.program_id(0)
    first = (t % tiles_per_seq) == 0
    last = (t % tiles_per_seq) == tiles_per_seq - 1
    xs_ref[0:HALO, :] = xp_ref[...].astype(bf16)
    xs_ref[HALO:HALO + ROW_TILE, :] = xm_ref[...].astype(bf16)
    xs_ref[HALO + ROW_TILE:, :] = xn_ref[...].astype(bf16)
    acc_ref[...] = jnp.zeros_like(acc_ref)

    def conv(h_ref, cw, cb):
        @pl.when(first)
        def _():
            h_ref[HALO - 8:HALO, :] = jnp.zeros((8, FF_CHUNK), f32)

        @pl.when(last)
        def _():
            h_ref[HALO + ROW_TILE:HALO + ROW_TILE + 8, :] = jnp.zeros((8, FF_CHUNK), f32)

        prev = h_ref[HALO - 1:HALO - 1 + ROW_TILE, :]
        cur = h_ref[HALO:HALO + ROW_TILE, :]
        nxt = h_ref[HALO + 1:HALO + 1 + ROW_TILE, :]
        return cw[0:1, :] * prev + cw[1:2, :] * cur + cw[2:3, :] * nxt + cb

    def chunk(c, carry):
        xs = xs_ref[...]
        ha_ref[...] = _dot(xs, wa_ref[c])
        hu_ref[...] = _dot(xs, wu_ref[c])
        a = conv(ha_ref, cwa_ref[c], cba_ref[c])
        u = conv(hu_ref, cwu_ref[c], cbu_ref[c])
        act = (_gelu(a) * u).astype(bf16)
        acc_ref[...] += _dot(act, wd_ref[c])
        return carry

    lax.fori_loop(0, N_FF_CHUNKS, chunk, 0)
    y_ref[...] = _layer_norm(ALPHA * xm_ref[...] + acc_ref[...], l2g_ref[...], l2b_ref[...])


def _ffn(x1, seq_len, wa, wu, cwa, cwu, cba, cbu, wd, ln2_g, ln2_b):
    T = x1.shape[0]
    tiles_per_seq = seq_len // ROW_TILE
    hb = ROW_TILE // HALO
    n_hb = T // HALO
    full = lambda *shape: pl.BlockSpec(shape, lambda t: (0,) * len(shape),
                                       pipeline_mode=pl.Buffered(1))
    m = ROW_TILE + 2 * HALO
    return pl.pallas_call(
        functools.partial(_ffn_kernel, tiles_per_seq=tiles_per_seq),
        grid=(T // ROW_TILE,),
        in_specs=[
            pl.BlockSpec((HALO, D_MODEL), lambda t: (jnp.maximum(t * hb - 1, 0), 0)),
            pl.BlockSpec((ROW_TILE, D_MODEL), lambda t: (t, 0)),
            pl.BlockSpec((HALO, D_MODEL), lambda t: (jnp.minimum((t + 1) * hb, n_hb - 1), 0)),
            full(N_FF_CHUNKS, D_MODEL, FF_CHUNK), full(N_FF_CHUNKS, D_MODEL, FF_CHUNK),
            full(N_FF_CHUNKS, 3, FF_CHUNK), full(N_FF_CHUNKS, 3, FF_CHUNK),
            full(N_FF_CHUNKS, 1, FF_CHUNK), full(N_FF_CHUNKS, 1, FF_CHUNK),
            full(N_FF_CHUNKS, FF_CHUNK, D_MODEL),
            full(1, D_MODEL), full(1, D_MODEL),
        ],
        out_specs=pl.BlockSpec((ROW_TILE, D_MODEL), lambda t: (t, 0)),
        out_shape=jax.ShapeDtypeStruct((T, D_MODEL), f32),
        scratch_shapes=[
            pltpu.VMEM((m, D_MODEL), bf16),
            pltpu.VMEM((m, FF_CHUNK), f32),
            pltpu.VMEM((m, FF_CHUNK), f32),
            pltpu.VMEM((ROW_TILE, D_MODEL), f32),
        ],
        compiler_params=pltpu.CompilerParams(
            dimension_semantics=("arbitrary",), vmem_limit_bytes=VMEM_LIMIT),
        name="ffn",
    )(x1, x1, x1, wa, wu, cwa, cwu, cba, cbu, wd, ln2_g, ln2_b)


def _t5_bucket(rel):
    nb = NUM_BUCKETS // 2
    ret = jnp.where(rel > 0, nb, 0)
    n = jnp.abs(rel)
    max_exact = nb // 2
    nf = jnp.maximum(n, 1).astype(f32)
    large = max_exact + (jnp.log(nf / max_exact) / math.log(MAX_DISTANCE / max_exact)
                         * (nb - max_exact)).astype(jnp.int32)
    large = jnp.minimum(large, nb - 1)
    return ret + jnp.where(n < max_exact, n, large)


def _band_bias(rel_table):
    rel = (jnp.arange(3 * BLOCK, dtype=jnp.int32)[None, :] - BLOCK) - jnp.arange(BLOCK, dtype=jnp.int32)[:, None]
    bias = rel_table[_t5_bucket(rel)].astype(f32).transpose(2, 0, 1)
    return jnp.where((jnp.abs(rel) <= WINDOW)[None], bias, NEG)


def _dft_matrices(n, scale):
    k = jnp.arange(n, dtype=jnp.int32)
    ang = ((k[:, None] * k[None, :]) % n).astype(f32) * (2.0 * math.pi / n)
    return jnp.cos(ang) * scale, jnp.sin(ang) * scale


def _encode(x, p):
    B, S, _ = x.shape
    T = B * S
    x2d = x.reshape(T, D_MODEL)
    qkv, f = _in_proj(x2d, p["ln_in_g"], p["ln_in_b"], p["w_in"])
    attn = _attention(qkv.reshape(B, S, QKV_COLS), p["bias"], p["sink"])
    z = _seq_dft(f.reshape(B, S, FOUR_W), p["cs"], p["sn"])
    x1 = _mix(x2d, attn.reshape(T, ATTN_W), z.reshape(T, 2 * FOUR_W), p["ccsc"], p["w_f"],
              p["g_attn"], p["g_four"], p["w_o"], p["ln_in_g"], p["ln_in_b"], p["ln1_g"], p["ln1_b"])
    y = _ffn(x1, S, p["wa"], p["wu"], p["cwa"], p["cwu"], p["cba"], p["cbu"], p["wd"],
             p["ln2_g"], p["ln2_b"])
    return y.reshape(B, S, D_MODEL)


def kernel(x_prompt, x_sample, ln_in_g, ln_in_b, rel_table, w_in, attn_sink, w_fourier, g_attn,
           g_fourier, w_o, ln1_g, ln1_b, w_up, conv_w, conv_b, w_down, ln2_g, ln2_b):
    S = x_prompt.shape[1]
    row = lambda v: v.reshape(1, -1).astype(f32)
    col_scale = jnp.concatenate([jnp.full((ATTN_W,), HEAD_DIM ** -0.5, f32),
                                 jnp.ones((IN_COLS - ATTN_W,), f32)])
    bias = _band_bias(rel_table)
    cs, sn = _dft_matrices(S, S ** -0.5)
    cc, sc = _dft_matrices(F_CH, F_CH ** -0.5)

    def chunks(w):
        lead = w.shape[:-1]
        return jnp.moveaxis(w.reshape(lead + (N_FF_CHUNKS, FF_CHUNK)), -2, 0)

    wup = w_up[0]
    cw = conv_w[0].astype(f32)
    cb = conv_b[0].reshape(1, -1).astype(f32)
    p = dict(
        ln_in_g=row(ln_in_g), ln_in_b=row(ln_in_b),
        w_in=(w_in[0] * col_scale).astype(bf16),
        bias=bias, sink=attn_sink[0].astype(f32),
        cs=cs.astype(bf16), sn=sn.astype(bf16),
        ccsc=jnp.concatenate([cc, -sc], axis=0).astype(bf16),
        w_f=w_fourier[0].astype(bf16),
        g_attn=row(g_attn[0]), g_four=row(g_fourier[0]),
        w_o=w_o[0].astype(bf16),
        ln1_g=row(ln1_g[0]), ln1_b=row(ln1_b[0]),
        wa=chunks(wup[:, :D_FF]).astype(bf16), wu=chunks(wup[:, D_FF:]).astype(bf16),
        cwa=chunks(cw[:, :D_FF]), cwu=chunks(cw[:, D_FF:]),
        cba=chunks(cb[:, :D_FF]), cbu=chunks(cb[:, D_FF:]),
        wd=w_down[0].reshape(N_FF_CHUNKS, FF_CHUNK, D_MODEL).astype(bf16),
        ln2_g=row(ln2_g[0]), ln2_b=row(ln2_b[0]),
    )
    return (_encode(x_prompt, p), _encode(x_sample, p))
```

```python
import functools
import math

import numpy as np
import jax
import jax.numpy as jnp
from jax import lax
from jax.experimental import pallas as pl
from jax.experimental.pallas import tpu as pltpu

D_MODEL = 1024
HEAD_DIM = 64
N_HEADS = 8
N_KV_HEADS = 2
GROUP = N_HEADS // N_KV_HEADS
WINDOW = 128
BLOCK = 128
NUM_BUCKETS = 32
MAX_DISTANCE = 128
F_GROUPS = 4
F_CH = 128
ATTN_W = N_HEADS * HEAD_DIM
FOUR_W = F_GROUPS * F_CH
KV_COLS = N_KV_HEADS * HEAD_DIM
QKV_COLS = ATTN_W + 2 * KV_COLS
IN_COLS = QKV_COLS + FOUR_W
D_FF = 2816
EPS = 1e-5
DEPTH = 1
ALPHA = (2.0 * DEPTH) ** 0.25
NEG = -1e30

ROW_TILE = 512
FF_CHUNK = 256
N_FF_CHUNKS = D_FF // FF_CHUNK
HALO = 16
DFT_ROW_TILE = 512
VMEM_LIMIT = 56 * 1024 * 1024

bf16 = jnp.bfloat16
f32 = jnp.float32


def _layer_norm(x, g, b):
    mu = jnp.mean(x, axis=-1, keepdims=True)
    xc = x - mu
    var = jnp.mean(xc * xc, axis=-1, keepdims=True)
    return xc * lax.rsqrt(var + EPS) * g + b


def _rms_norm(x, g):
    return x * lax.rsqrt(jnp.mean(x * x, axis=-1, keepdims=True) + EPS) * g


def _dot(a, b):
    return jnp.dot(a, b, preferred_element_type=f32)


def _in_proj_kernel(x_ref, g_ref, b_ref, w_ref, qkv_ref, f_ref):
    y = _layer_norm(x_ref[...], g_ref[...], b_ref[...])
    h = _dot(y.astype(bf16), w_ref[...])
    qkv_ref[...] = h[:, :QKV_COLS].astype(bf16)
    f_ref[...] = h[:, QKV_COLS:].astype(bf16)


def _in_proj(x2d, ln_g, ln_b, w_in):
    T = x2d.shape[0]
    return pl.pallas_call(
        _in_proj_kernel,
        grid=(T // ROW_TILE,),
        in_specs=[
            pl.BlockSpec((ROW_TILE, D_MODEL), lambda t: (t, 0)),
            pl.BlockSpec((1, D_MODEL), lambda t: (0, 0)),
            pl.BlockSpec((1, D_MODEL), lambda t: (0, 0)),
            pl.BlockSpec((D_MODEL, IN_COLS), lambda t: (0, 0)),
        ],
        out_specs=[
            pl.BlockSpec((ROW_TILE, QKV_COLS), lambda t: (t, 0)),
            pl.BlockSpec((ROW_TILE, FOUR_W), lambda t: (t, 0)),
        ],
        out_shape=[
            jax.ShapeDtypeStruct((T, QKV_COLS), bf16),
            jax.ShapeDtypeStruct((T, FOUR_W), bf16),
        ],
        compiler_params=pltpu.CompilerParams(
            dimension_semantics=("arbitrary",), vmem_limit_bytes=VMEM_LIMIT),
        name="in_proj",
    )(x2d, ln_g, ln_b, w_in)


def _attn_kernel(q_ref, k_ref, v_ref, bias_ref, sink_ref, o_ref, *, nblk):
    i = pl.program_id(1)
    seq = nblk * BLOCK
    starts = [
        pl.multiple_of(jnp.clip((i + c) * BLOCK, 0, seq - BLOCK), BLOCK) for c in (-1, 0, 1)
    ]
    col = lax.broadcasted_iota(jnp.int32, (BLOCK, 3 * BLOCK), 1)
    lo = jnp.where(i == 0, BLOCK, 0)
    hi = jnp.where(i == nblk - 1, 2 * BLOCK, 3 * BLOCK)
    in_seq = (col >= lo) & (col < hi)

    k_win = jnp.concatenate([k_ref[0, pl.ds(s, BLOCK), :] for s in starts], axis=0)
    v_win = jnp.concatenate([v_ref[0, pl.ds(s, BLOCK), :] for s in starts], axis=0)
    q = q_ref[0]
    outs = []
    for h in range(N_HEADS):
        g = h // GROUP
        q_h = q[:, h * HEAD_DIM:(h + 1) * HEAD_DIM]
        k_g = k_win[:, g * HEAD_DIM:(g + 1) * HEAD_DIM]
        v_g = v_win[:, g * HEAD_DIM:(g + 1) * HEAD_DIM]
        logits = lax.dot_general(q_h, k_g, (((1,), (1,)), ((), ())),
                                 preferred_element_type=f32)
        logits = jnp.where(in_seq, logits + bias_ref[h], NEG)
        sink = sink_ref[h]
        m = jnp.maximum(jnp.max(logits, axis=-1, keepdims=True), sink)
        p = jnp.exp(logits - m)
        denom = jnp.sum(p, axis=-1, keepdims=True) + jnp.exp(sink - m)
        pv = _dot(p.astype(bf16), v_g)
        outs.append(pv / denom)
    o_ref[0] = jnp.concatenate(outs, axis=-1).astype(bf16)


def _attention(qkv, bias, sink):
    B, S, _ = qkv.shape
    nblk = S // BLOCK
    k_col = ATTN_W // KV_COLS
    return pl.pallas_call(
        functools.partial(_attn_kernel, nblk=nblk),
        grid=(B, nblk),
        in_specs=[
            pl.BlockSpec((1, BLOCK, ATTN_W), lambda b, i: (b, i, 0)),
            pl.BlockSpec((1, S, KV_COLS), lambda b, i: (b, 0, k_col)),
            pl.BlockSpec((1, S, KV_COLS), lambda b, i: (b, 0, k_col + 1)),
            pl.BlockSpec((N_HEADS, BLOCK, 3 * BLOCK), lambda b, i: (0, 0, 0)),
            pl.BlockSpec(memory_space=pltpu.SMEM),
        ],
        out_specs=pl.BlockSpec((1, BLOCK, ATTN_W), lambda b, i: (b, i, 0)),
        out_shape=jax.ShapeDtypeStruct((B, S, ATTN_W), bf16),
        compiler_params=pltpu.CompilerParams(
            dimension_semantics=("arbitrary", "arbitrary"), vmem_limit_bytes=VMEM_LIMIT),
        name="attn",
    )(qkv, qkv, qkv, bias, sink)


def _seq_dft_kernel(c_ref, s_ref, f_ref, z_ref):
    f = f_ref[0]
    a = _dot(c_ref[...], f)
    p = _dot(s_ref[...], f)
    for g in range(F_GROUPS):
        lo = g * F_CH
        z_ref[0, :, 2 * lo:2 * lo + F_CH] = a[:, lo:lo + F_CH].astype(bf16)
        z_ref[0, :, 2 * lo + F_CH:2 * lo + 2 * F_CH] = p[:, lo:lo + F_CH].astype(bf16)


def _seq_dft(f3d, cs, sn):
    B, S, _ = f3d.shape
    return pl.pallas_call(
        _seq_dft_kernel,
        grid=(S // DFT_ROW_TILE, B),
        in_specs=[
            pl.BlockSpec((DFT_ROW_TILE, S), lambda r, b: (r, 0)),
            pl.BlockSpec((DFT_ROW_TILE, S), lambda r, b: (r, 0)),
            pl.BlockSpec((1, S, FOUR_W), lambda r, b: (b, 0, 0)),
        ],
        out_specs=pl.BlockSpec((1, DFT_ROW_TILE, 2 * FOUR_W), lambda r, b: (b, r, 0)),
        out_shape=jax.ShapeDtypeStruct((B, S, 2 * FOUR_W), bf16),
        compiler_params=pltpu.CompilerParams(
            dimension_semantics=("arbitrary", "arbitrary"), vmem_limit_bytes=VMEM_LIMIT),
        name="seq_dft",
    )(cs, sn, f3d)


def _mix_kernel(x_ref, attn_ref, z_ref, ccsc_ref, wf_ref, ga_ref, gf_ref, wo_ref,
                lg_ref, lb_ref, l1g_ref, l1b_ref, x1_ref):
    x0 = _layer_norm(x_ref[...], lg_ref[...], lb_ref[...])
    an = _rms_norm(attn_ref[...].astype(f32), ga_ref[...]).astype(bf16)
    z = z_ref[...]
    four = []
    for g in range(F_GROUPS):
        zc = _dot(z[:, 2 * g * F_CH:2 * (g + 1) * F_CH], ccsc_ref[...])
        four.append(_dot(zc.astype(bf16), wf_ref[g]))
    fn = _rms_norm(jnp.concatenate(four, axis=-1), gf_ref[...]).astype(bf16)
    mix = _dot(an, wo_ref[:ATTN_W, :]) + _dot(fn, wo_ref[ATTN_W:, :])
    x1_ref[...] = _layer_norm(ALPHA * x0 + mix, l1g_ref[...], l1b_ref[...])


def _mix(x2d, attn2d, z2d, ccsc, w_f, g_attn, g_four, w_o, ln_g, ln_b, ln1_g, ln1_b):
    T = x2d.shape[0]
    row = lambda w: pl.BlockSpec((ROW_TILE, w), lambda t: (t, 0))
    full = lambda *shape: pl.BlockSpec(shape, lambda t: (0,) * len(shape))
    return pl.pallas_call(
        _mix_kernel,
        grid=(T // ROW_TILE,),
        in_specs=[
            row(D_MODEL), row(ATTN_W), row(2 * FOUR_W),
            full(2 * F_CH, F_CH), full(F_GROUPS, F_CH, F_CH),
            full(1, ATTN_W), full(1, FOUR_W), full(D_MODEL, D_MODEL),
            full(1, D_MODEL), full(1, D_MODEL), full(1, D_MODEL), full(1, D_MODEL),
        ],
        out_specs=row(D_MODEL),
        out_shape=jax.ShapeDtypeStruct((T, D_MODEL), f32),
        compiler_params=pltpu.CompilerParams(
            dimension_semantics=("arbitrary",), vmem_limit_bytes=VMEM_LIMIT),
        name="mix",
    )(x2d, attn2d, z2d, ccsc, w_f, g_attn, g_four, w_o, ln_g, ln_b, ln1_g, ln1_b)


def _gelu(x):
    return 0.5 * x * (1.0 + lax.erf(x * (1.0 / math.sqrt(2.0))))


def _ffn_kernel(xp_ref, xm_ref, xn_ref, wup_ref, cw_ref, cb_ref, wd_ref, l2g_ref, l2b_ref,
                y_ref, xs_ref, h_ref, act_ref, *, tiles_per_seq):
    t = pl.program_id(0)
    first = (t % tiles_per_seq) == 0
    last = (t % tiles_per_seq) == tiles_per_seq - 1
    xs_ref[0:HALO, :] = jnp.where(first, 0.0, xp_ref[...]).astype(bf16)
    xs_ref[HALO:HALO + ROW_TILE, :] = xm_ref[...].astype(bf16)
    xs_ref[HALO + ROW_TILE:, :] = jnp.where(last, 0.0, xn_ref[...]).astype(bf16)

    def conv(slot, col):
        cw = cw_ref[:, col:col + FF_CHUNK]
        prev = h_ref[slot, HALO - 1:HALO - 1 + ROW_TILE, :]
        cur = h_ref[slot, HALO:HALO + ROW_TILE, :]
        nxt = h_ref[slot, HALO + 1:HALO + 1 + ROW_TILE, :]
        return (cw[0:1, :] * prev + cw[1:2, :] * cur + cw[2:3, :] * nxt
                + cb_ref[:, col:col + FF_CHUNK])

    for c in range(N_FF_CHUNKS):
        sa, su = 2 * (c % 2), 2 * (c % 2) + 1
        col_a, col_u = c * FF_CHUNK, D_FF + c * FF_CHUNK
        h_ref[sa] = _dot(xs_ref[...], wup_ref[:, col_a:col_a + FF_CHUNK])
        h_ref[su] = _dot(xs_ref[...], wup_ref[:, col_u:col_u + FF_CHUNK])
        act = _gelu(conv(sa, col_a)) * conv(su, col_u)
        act_ref[:, col_a:col_a + FF_CHUNK] = act.astype(bf16)
    ffn = _dot(act_ref[...], wd_ref[...])
    y_ref[...] = _layer_norm(ALPHA * xm_ref[...] + ffn, l2g_ref[...], l2b_ref[...])


def _ffn(x1, seq_len, w_up, conv_w, conv_b, w_down, ln2_g, ln2_b):
    T = x1.shape[0]
    tiles_per_seq = seq_len // ROW_TILE
    hb = ROW_TILE // HALO
    n_hb = T // HALO
    full = lambda *shape: pl.BlockSpec(shape, lambda t: (0,) * len(shape),
                                       pipeline_mode=pl.Buffered(1))
    m = ROW_TILE + 2 * HALO
    return pl.pallas_call(
        functools.partial(_ffn_kernel, tiles_per_seq=tiles_per_seq),
        grid=(T // ROW_TILE,),
        in_specs=[
            pl.BlockSpec((HALO, D_MODEL), lambda t: (jnp.maximum(t * hb - 1, 0), 0)),
            pl.BlockSpec((ROW_TILE, D_MODEL), lambda t: (t, 0)),
            pl.BlockSpec((HALO, D_MODEL), lambda t: (jnp.minimum((t + 1) * hb, n_hb - 1), 0)),
            full(D_MODEL, 2 * D_FF), full(3, 2 * D_FF), full(1, 2 * D_FF),
            full(D_FF, D_MODEL), full(1, D_MODEL), full(1, D_MODEL),
        ],
        out_specs=pl.BlockSpec((ROW_TILE, D_MODEL), lambda t: (t, 0)),
        out_shape=jax.ShapeDtypeStruct((T, D_MODEL), f32),
        scratch_shapes=[
            pltpu.VMEM((m, D_MODEL), bf16),
            pltpu.VMEM((4, m, FF_CHUNK), f32),
            pltpu.VMEM((ROW_TILE, D_FF), bf16),
        ],
        compiler_params=pltpu.CompilerParams(
            dimension_semantics=("arbitrary",), vmem_limit_bytes=VMEM_LIMIT),
        name="ffn",
    )(x1, x1, x1, w_up, conv_w, conv_b, w_down, ln2_g, ln2_b)


def _t5_bucket(rel):
    nb = NUM_BUCKETS // 2
    ret = jnp.where(rel > 0, nb, 0)
    n = jnp.abs(rel)
    max_exact = nb // 2
    nf = jnp.maximum(n, 1).astype(f32)
    large = max_exact + (jnp.log(nf / max_exact) / math.log(MAX_DISTANCE / max_exact)
                         * (nb - max_exact)).astype(jnp.int32)
    large = jnp.minimum(large, nb - 1)
    return ret + jnp.where(n < max_exact, n, large)


def _band_bias(rel_table):
    rel = (jnp.arange(3 * BLOCK, dtype=jnp.int32)[None, :] - BLOCK) - jnp.arange(BLOCK, dtype=jnp.int32)[:, None]
    bias = rel_table[_t5_bucket(rel)].astype(f32).transpose(2, 0, 1)
    return jnp.where((jnp.abs(rel) <= WINDOW)[None], bias, NEG)


def _dft_matrices(n, scale):
    k = jnp.arange(n, dtype=jnp.int32)
    ang = ((k[:, None] * k[None, :]) % n).astype(f32) * (2.0 * math.pi / n)
    return jnp.cos(ang) * scale, jnp.sin(ang) * scale


def _encode(x, p):
    B, S, _ = x.shape
    T = B * S
    x2d = x.reshape(T, D_MODEL)
    qkv, f = _in_proj(x2d, p["ln_in_g"], p["ln_in_b"], p["w_in"])
    attn = _attention(qkv.reshape(B, S, QKV_COLS), p["bias"], p["sink"])
    z = _seq_dft(f.reshape(B, S, FOUR_W), p["cs"], p["sn"])
    x1 = _mix(x2d, attn.reshape(T, ATTN_W), z.reshape(T, 2 * FOUR_W), p["ccsc"], p["w_f"],
              p["g_attn"], p["g_four"], p["w_o"], p["ln_in_g"], p["ln_in_b"], p["ln1_g"], p["ln1_b"])
    y = _ffn(x1, S, p["w_up"], p["conv_w"], p["conv_b"], p["w_down"], p["ln2_g"], p["ln2_b"])
    return y.reshape(B, S, D_MODEL)


def kernel(x_prompt, x_sample, ln_in_g, ln_in_b, rel_table, w_in, attn_sink, w_fourier, g_attn,
           g_fourier, w_o, ln1_g, ln1_b, w_up, conv_w, conv_b, w_down, ln2_g, ln2_b):
    S = x_prompt.shape[1]
    row = lambda v: v.reshape(1, -1).astype(f32)
    col_scale = jnp.concatenate([jnp.full((ATTN_W,), HEAD_DIM ** -0.5, f32),
                                 jnp.ones((IN_COLS - ATTN_W,), f32)])
    bias = _band_bias(rel_table)
    cs, sn = _dft_matrices(S, S ** -0.5)
    cc, sc = _dft_matrices(F_CH, F_CH ** -0.5)

    p = dict(
        ln_in_g=row(ln_in_g), ln_in_b=row(ln_in_b),
        w_in=(w_in[0] * col_scale).astype(bf16),
        bias=bias, sink=attn_sink[0].astype(f32),
        cs=cs.astype(bf16), sn=sn.astype(bf16),
        ccsc=jnp.concatenate([cc, -sc], axis=0).astype(bf16),
        w_f=w_fourier[0].astype(bf16),
        g_attn=row(g_attn[0]), g_four=row(g_fourier[0]),
        w_o=w_o[0].astype(bf16),
        ln1_g=row(ln1_g[0]), ln1_b=row(ln1_b[0]),
        w_up=w_up[0].astype(bf16), conv_w=conv_w[0].astype(f32), conv_b=row(conv_b[0]),
        w_down=w_down[0].astype(bf16),
        ln2_g=row(ln2_g[0]), ln2_b=row(ln2_b[0]),
    )
    return (_encode(x_prompt, p), _encode(x_sample, p))
```

```python
import functools
import math

import jax
import jax.numpy as jnp
from jax import lax
from jax.experimental import pallas as pl
from jax.experimental.pallas import tpu as pltpu

D_MODEL = 1024
HEAD_DIM = 64
N_HEADS = 8
N_KV_HEADS = 2
GROUP = N_HEADS // N_KV_HEADS
WINDOW = 128
BLOCK = 128
NUM_BUCKETS = 32
MAX_DISTANCE = 128
F_GROUPS = 4
F_CH = 128
ATTN_W = N_HEADS * HEAD_DIM
FOUR_W = F_GROUPS * F_CH
KV_COLS = N_KV_HEADS * HEAD_DIM
QKV_COLS = ATTN_W + 2 * KV_COLS
IN_COLS = QKV_COLS + FOUR_W
D_FF = 2816
EPS = 1e-5
DEPTH = 1
ALPHA = (2.0 * DEPTH) ** 0.25
NEG = -1e30

ROW_TILE = 512
FF_CHUNK = 256
N_FF_CHUNKS = D_FF // FF_CHUNK
HALO = 16
RADIX = 64
JB = 16
SUB = 8
VMEM_LIMIT = 56 * 1024 * 1024

bf16 = jnp.bfloat16
f32 = jnp.float32


def _layer_norm(x, g, b):
    mu = jnp.mean(x, axis=-1, keepdims=True)
    xc = x - mu
    var = jnp.mean(xc * xc, axis=-1, keepdims=True)
    return xc * lax.rsqrt(var + EPS) * g + b


def _rms_norm(x, g):
    return x * lax.rsqrt(jnp.mean(x * x, axis=-1, keepdims=True) + EPS) * g


def _dot(a, b):
    return jnp.dot(a, b, preferred_element_type=f32)


def _in_proj_kernel(x_ref, g_ref, b_ref, w_ref, dft_ref, qkv_ref, y_ref):
    x = x_ref[0].reshape(RADIX * JB, D_MODEL)
    h = _dot(_layer_norm(x, g_ref[...], b_ref[...]).astype(bf16), w_ref[...])
    qkv_ref[0] = h[:, :QKV_COLS].astype(bf16).reshape(RADIX, JB, QKV_COLS)
    f = h[:, QKV_COLS:].reshape(RADIX, JB, FOUR_W)
    for hh in range(JB // SUB):
        fh = f[:, hh * SUB:(hh + 1) * SUB, :].reshape(RADIX * SUB, FOUR_W).astype(bf16)
        y = _dot(dft_ref[0, hh], fh)
        y_ref[0, :, hh * SUB:(hh + 1) * SUB] = y.reshape(2, SUB, RADIX, FOUR_W)


def _in_proj(x, ln_g, ln_b, w_in, dft1):
    B = x.shape[0]
    x4 = x.reshape(B, RADIX, RADIX, D_MODEL)
    return pl.pallas_call(
        _in_proj_kernel,
        grid=(RADIX // JB, B),
        in_specs=[
            pl.BlockSpec((1, RADIX, JB, D_MODEL), lambda jb, b: (b, 0, jb, 0)),
            pl.BlockSpec((1, D_MODEL), lambda jb, b: (0, 0)),
            pl.BlockSpec((1, D_MODEL), lambda jb, b: (0, 0)),
            pl.BlockSpec((D_MODEL, IN_COLS), lambda jb, b: (0, 0)),
            pl.BlockSpec((1, JB // SUB, 2 * RADIX * SUB, RADIX * SUB), lambda jb, b: (jb, 0, 0, 0)),
        ],
        out_specs=[
            pl.BlockSpec((1, RADIX, JB, QKV_COLS), lambda jb, b: (b, 0, jb, 0)),
            pl.BlockSpec((1, 2, JB, RADIX, FOUR_W), lambda jb, b: (b, 0, jb, 0, 0)),
        ],
        out_shape=[
            jax.ShapeDtypeStruct((B, RADIX, RADIX, QKV_COLS), bf16),
            jax.ShapeDtypeStruct((B, 2, RADIX, RADIX, FOUR_W), f32),
        ],
        compiler_params=pltpu.CompilerParams(
            dimension_semantics=("arbitrary", "arbitrary"), vmem_limit_bytes=VMEM_LIMIT),
        name="in_proj",
    )(x4, ln_g, ln_b, w_in, dft1)


def _attn_kernel(q_ref, k_ref, v_ref, bias_ref, sink_ref, o_ref, *, nblk):
    i = pl.program_id(1)
    seq = nblk * BLOCK
    starts = [
        pl.multiple_of(jnp.clip((i + c) * BLOCK, 0, seq - BLOCK), BLOCK) for c in (-1, 0, 1)
    ]
    col = lax.broadcasted_iota(jnp.int32, (BLOCK, 3 * BLOCK), 1)
    lo = jnp.where(i == 0, BLOCK, 0)
    hi = jnp.where(i == nblk - 1, 2 * BLOCK, 3 * BLOCK)
    in_seq = (col >= lo) & (col < hi)

    k_win = jnp.concatenate([k_ref[0, pl.ds(s, BLOCK), :] for s in starts], axis=0)
    v_win = jnp.concatenate([v_ref[0, pl.ds(s, BLOCK), :] for s in starts], axis=0)
    q = q_ref[0]
    outs = []
    for h in range(N_HEADS):
        g = h // GROUP
        q_h = q[:, h * HEAD_DIM:(h + 1) * HEAD_DIM]
        k_g = k_win[:, g * HEAD_DIM:(g + 1) * HEAD_DIM]
        v_g = v_win[:, g * HEAD_DIM:(g + 1) * HEAD_DIM]
        logits = lax.dot_general(q_h, k_g, (((1,), (1,)), ((), ())),
                                 preferred_element_type=f32)
        logits = jnp.where(in_seq, logits + bias_ref[h], NEG)
        sink = sink_ref[h]
        m = jnp.maximum(jnp.max(logits, axis=-1, keepdims=True), sink)
        p = jnp.exp(logits - m)
        denom = jnp.sum(p, axis=-1, keepdims=True) + jnp.exp(sink - m)
        pv = _dot(p.astype(bf16), v_g)
        outs.append(pv / denom)
    o_ref[0] = jnp.concatenate(outs, axis=-1).astype(bf16)


def _attention(qkv, bias, sink):
    B, S, _ = qkv.shape
    nblk = S // BLOCK
    k_col = ATTN_W // KV_COLS
    return pl.pallas_call(
        functools.partial(_attn_kernel, nblk=nblk),
        grid=(B, nblk),
        in_specs=[
            pl.BlockSpec((1, BLOCK, ATTN_W), lambda b, i: (b, i, 0)),
            pl.BlockSpec((1, S, KV_COLS), lambda b, i: (b, 0, k_col)),
            pl.BlockSpec((1, S, KV_COLS), lambda b, i: (b, 0, k_col + 1)),
            pl.BlockSpec((N_HEADS, BLOCK, 3 * BLOCK), lambda b, i: (0, 0, 0)),
            pl.BlockSpec(memory_space=pltpu.SMEM),
        ],
        out_specs=pl.BlockSpec((1, BLOCK, ATTN_W), lambda b, i: (b, i, 0)),
        out_shape=jax.ShapeDtypeStruct((B, S, ATTN_W), bf16),
        compiler_params=pltpu.CompilerParams(
            dimension_semantics=("arbitrary", "arbitrary"), vmem_limit_bytes=VMEM_LIMIT),
        name="attn",
    )(qkv, qkv, qkv, bias, sink)


def _four_kernel(y_ref, ha_ref, hp_ref, cc_ref, nsc_ref, wf_ref, o_ref):
    y = y_ref[0]
    halves = []
    for hh in range(JB // SUB):
        ys = y[:, :, hh * SUB:(hh + 1) * SUB, :].reshape(2 * RADIX * SUB, FOUR_W).astype(bf16)
        a = _dot(ha_ref[...], ys).astype(bf16)
        p = _dot(hp_ref[...], ys).astype(bf16)
        four = []
        for g in range(F_GROUPS):
            cols = slice(g * F_CH, (g + 1) * F_CH)
            zc = _dot(a[:, cols], cc_ref[...]) + _dot(p[:, cols], nsc_ref[...])
            four.append(_dot(zc.astype(bf16), wf_ref[g]))
        halves.append(jnp.concatenate(four, axis=-1).reshape(RADIX, SUB, FOUR_W))
    o_ref[0] = jnp.concatenate(halves, axis=1).astype(bf16)


def _fourier(y, ha, hp, cc, nsc, w_f):
    B = y.shape[0]
    full = lambda *shape: pl.BlockSpec(shape, lambda b, kb: (0,) * len(shape))
    return pl.pallas_call(
        _four_kernel,
        grid=(B, RADIX // JB),
        in_specs=[
            pl.BlockSpec((1, 2, RADIX, JB, FOUR_W), lambda b, kb: (b, 0, 0, kb, 0)),
            full(RADIX * SUB, 2 * RADIX * SUB), full(RADIX * SUB, 2 * RADIX * SUB),
            full(F_CH, F_CH), full(F_CH, F_CH), full(F_GROUPS, F_CH, F_CH),
        ],
        out_specs=pl.BlockSpec((1, RADIX, JB, FOUR_W), lambda b, kb: (b, 0, kb, 0)),
        out_shape=jax.ShapeDtypeStruct((B, RADIX, RADIX, FOUR_W), bf16),
        compiler_params=pltpu.CompilerParams(
            dimension_semantics=("arbitrary", "arbitrary"), vmem_limit_bytes=VMEM_LIMIT),
        name="fourier",
    )(y, ha, hp, cc, nsc, w_f)


def _mix_kernel(x_ref, attn_ref, four_ref, ga_ref, gf_ref, wo_ref,
                lg_ref, lb_ref, l1g_ref, l1b_ref, x1_ref):
    x0 = _layer_norm(x_ref[...], lg_ref[...], lb_ref[...])
    an = _rms_norm(attn_ref[...].astype(f32), ga_ref[...]).astype(bf16)
    fn = _rms_norm(four_ref[...].astype(f32), gf_ref[...]).astype(bf16)
    mix = _dot(an, wo_ref[:ATTN_W, :]) + _dot(fn, wo_ref[ATTN_W:, :])
    x1_ref[...] = _layer_norm(ALPHA * x0 + mix, l1g_ref[...], l1b_ref[...])


def _mix(x2d, attn2d, four2d, g_attn, g_four, w_o, ln_g, ln_b, ln1_g, ln1_b):
    T = x2d.shape[0]
    row = lambda w: pl.BlockSpec((ROW_TILE, w), lambda t: (t, 0))
    full = lambda *shape: pl.BlockSpec(shape, lambda t: (0,) * len(shape))
    return pl.pallas_call(
        _mix_kernel,
        grid=(T // ROW_TILE,),
        in_specs=[
            row(D_MODEL), row(ATTN_W), row(FOUR_W),
            full(1, ATTN_W), full(1, FOUR_W), full(D_MODEL, D_MODEL),
            full(1, D_MODEL), full(1, D_MODEL), full(1, D_MODEL), full(1, D_MODEL),
        ],
        out_specs=row(D_MODEL),
        out_shape=jax.ShapeDtypeStruct((T, D_MODEL), f32),
        compiler_params=pltpu.CompilerParams(
            dimension_semantics=("arbitrary",), vmem_limit_bytes=VMEM_LIMIT),
        name="mix",
    )(x2d, attn2d, four2d, g_attn, g_four, w_o, ln_g, ln_b, ln1_g, ln1_b)


def _gelu(x):
    return 0.5 * x * (1.0 + lax.erf(x * (1.0 / math.sqrt(2.0))))


def _ffn_kernel(xp_ref, xm_ref, xn_ref, wup_ref, cw_ref, cb_ref, wd_ref, l2g_ref, l2b_ref,
                y_ref, xs_ref, h_ref, act_ref, *, tiles_per_seq):
    t = pl.program_id(0)
    first = (t % tiles_per_seq) == 0
    last = (t % tiles_per_seq) == tiles_per_seq - 1
    xs_ref[0:HALO, :] = jnp.where(first, 0.0, xp_ref[...]).astype(bf16)
    xs_ref[HALO:HALO + ROW_TILE, :] = xm_ref[...].astype(bf16)
    xs_ref[HALO + ROW_TILE:, :] = jnp.where(last, 0.0, xn_ref[...]).astype(bf16)

    def conv(slot, col):
        cw = cw_ref[:, col:col + FF_CHUNK]
        prev = h_ref[slot, HALO - 1:HALO - 1 + ROW_TILE, :]
        cur = h_ref[slot, HALO:HALO + ROW_TILE, :]
        nxt = h_ref[slot, HALO + 1:HALO + 1 + ROW_TILE, :]
        return (cw[0:1, :] * prev + cw[1:2, :] * cur + cw[2:3, :] * nxt
                + cb_ref[:, col:col + FF_CHUNK])

    for c in range(N_FF_CHUNKS):
        sa, su = 2 * (c % 2), 2 * (c % 2) + 1
        col_a, col_u = c * FF_CHUNK, D_FF + c * FF_CHUNK
        h_ref[sa] = _dot(xs_ref[...], wup_ref[:, col_a:col_a + FF_CHUNK])
        h_ref[su] = _dot(xs_ref[...], wup_ref[:, col_u:col_u + FF_CHUNK])
        act = _gelu(conv(sa, col_a)) * conv(su, col_u)
        act_ref[:, col_a:col_a + FF_CHUNK] = act.astype(bf16)
    ffn = _dot(act_ref[...], wd_ref[...])
    y_ref[...] = _layer_norm(ALPHA * xm_ref[...] + ffn, l2g_ref[...], l2b_ref[...])


def _ffn(x1, seq_len, w_up, conv_w, conv_b, w_down, ln2_g, ln2_b):
    T = x1.shape[0]
    tiles_per_seq = seq_len // ROW_TILE
    hb = ROW_TILE // HALO
    n_hb = T // HALO
    full = lambda *shape: pl.BlockSpec(shape, lambda t: (0,) * len(shape),
                                       pipeline_mode=pl.Buffered(1))
    m = ROW_TILE + 2 * HALO
    return pl.pallas_call(
        functools.partial(_ffn_kernel, tiles_per_seq=tiles_per_seq),
        grid=(T // ROW_TILE,),
        in_specs=[
            pl.BlockSpec((HALO, D_MODEL), lambda t: (jnp.maximum(t * hb - 1, 0), 0)),
            pl.BlockSpec((ROW_TILE, D_MODEL), lambda t: (t, 0)),
            pl.BlockSpec((HALO, D_MODEL), lambda t: (jnp.minimum((t + 1) * hb, n_hb - 1), 0)),
            full(D_MODEL, 2 * D_FF), full(3, 2 * D_FF), full(1, 2 * D_FF),
            full(D_FF, D_MODEL), full(1, D_MODEL), full(1, D_MODEL),
        ],
        out_specs=pl.BlockSpec((ROW_TILE, D_MODEL), lambda t: (t, 0)),
        out_shape=jax.ShapeDtypeStruct((T, D_MODEL), f32),
        scratch_shapes=[
            pltpu.VMEM((m, D_MODEL), bf16),
            pltpu.VMEM((4, m, FF_CHUNK), f32),
            pltpu.VMEM((ROW_TILE, D_FF), bf16),
        ],
        compiler_params=pltpu.CompilerParams(
            dimension_semantics=("arbitrary",), vmem_limit_bytes=VMEM_LIMIT),
        name="ffn",
    )(x1, x1, x1, w_up, conv_w, conv_b, w_down, ln2_g, ln2_b)


def _t5_bucket(rel):
    nb = NUM_BUCKETS // 2
    ret = jnp.where(rel > 0, nb, 0)
    n = jnp.abs(rel)
    max_exact = nb // 2
    nf = jnp.maximum(n, 1).astype(f32)
    large = max_exact + (jnp.log(nf / max_exact) / math.log(MAX_DISTANCE / max_exact)
                         * (nb - max_exact)).astype(jnp.int32)
    large = jnp.minimum(large, nb - 1)
    return ret + jnp.where(n < max_exact, n, large)


def _band_bias(rel_table):
    rel = (jnp.arange(3 * BLOCK, dtype=jnp.int32)[None, :] - BLOCK) - jnp.arange(BLOCK, dtype=jnp.int32)[:, None]
    bias = rel_table[_t5_bucket(rel)].astype(f32).transpose(2, 0, 1)
    return jnp.where((jnp.abs(rel) <= WINDOW)[None], bias, NEG)


def _trig(num, den):
    ang = (num % den).astype(f32) * (2.0 * math.pi / den)
    return jnp.cos(ang), jnp.sin(ang)


def _stage1_tables(seq):
    ar = lambda n: jnp.arange(n, dtype=jnp.int32)
    j = (ar(RADIX // JB)[:, None, None] * JB + ar(JB // SUB)[None, :, None] * SUB + ar(SUB)[None, None, :])
    n = RADIX * ar(RADIX)[None, None, None, None, :] + j[:, :, :, None, None]
    c, s = _trig(ar(RADIX)[None, None, None, :, None] * n, seq)
    val = jnp.stack([c, s], axis=2) * (seq ** -0.5)
    g = val[..., None] * jnp.eye(SUB, dtype=f32)[None, None, None, :, None, None, :]
    return g.reshape(RADIX // JB, JB // SUB, 2 * SUB * RADIX, RADIX * SUB).astype(bf16)


def _stage2_tables():
    ar = jnp.arange(RADIX, dtype=jnp.int32)
    c, s = _trig(ar[:, None] * ar[None, :], RADIX)
    eye = jnp.eye(SUB, dtype=f32)

    def expand(parts):
        m = jnp.stack(parts)[:, :, :, None, None] * eye[None, None, None]
        return m.transpose(1, 3, 0, 2, 4).reshape(RADIX * SUB, 2 * RADIX * SUB).astype(bf16)

    return expand([c, -s]), expand([s, c])


def _encode(x, p):
    B, S, _ = x.shape
    assert S == RADIX * RADIX
    T = B * S
    x2d = x.reshape(T, D_MODEL)
    qkv, y1 = _in_proj(x, p["ln_in_g"], p["ln_in_b"], p["w_in"], p["dft1"])
    attn = _attention(qkv.reshape(B, S, QKV_COLS), p["bias"], p["sink"])
    four = _fourier(y1, p["ha"], p["hp"], p["cc"], p["nsc"], p["w_f"])
    x1 = _mix(x2d, attn.reshape(T, ATTN_W), four.reshape(T, FOUR_W),
              p["g_attn"], p["g_four"], p["w_o"], p["ln_in_g"], p["ln_in_b"], p["ln1_g"], p["ln1_b"])
    y = _ffn(x1, S, p["w_up"], p["conv_w"], p["conv_b"], p["w_down"], p["ln2_g"], p["ln2_b"])
    return y.reshape(B, S, D_MODEL)


def kernel(x_prompt, x_sample, ln_in_g, ln_in_b, rel_table, w_in, attn_sink, w_fourier, g_attn,
           g_fourier, w_o, ln1_g, ln1_b, w_up, conv_w, conv_b, w_down, ln2_g, ln2_b):
    S = x_prompt.shape[1]
    row = lambda v: v.reshape(1, -1).astype(f32)
    col_scale = jnp.concatenate([jnp.full((ATTN_W,), HEAD_DIM ** -0.5, f32),
                                 jnp.ones((IN_COLS - ATTN_W,), f32)])
    ar = jnp.arange(F_CH, dtype=jnp.int32)
    cc, sc = _trig(ar[:, None] * ar[None, :], F_CH)
    ha, hp = _stage2_tables()

    p = dict(
        ln_in_g=row(ln_in_g), ln_in_b=row(ln_in_b),
        w_in=(w_in[0] * col_scale).astype(bf16),
        bias=_band_bias(rel_table), sink=attn_sink[0].astype(f32),
        dft1=_stage1_tables(S), ha=ha, hp=hp,
        cc=(cc * F_CH ** -0.5).astype(bf16), nsc=(-sc * F_CH ** -0.5).astype(bf16),
        w_f=w_fourier[0].astype(bf16),
        g_attn=row(g_attn[0]), g_four=row(g_fourier[0]),
        w_o=w_o[0].astype(bf16),
        ln1_g=row(ln1_g[0]), ln1_b=row(ln1_b[0]),
        w_up=w_up[0].astype(bf16), conv_w=conv_w[0].astype(f32), conv_b=row(conv_b[0]),
        w_down=w_down[0].astype(bf16),
        ln2_g=row(ln2_g[0]), ln2_b=row(ln2_b[0]),
    )
    return (_encode(x_prompt, p), _encode(x_sample, p))
```

```python
import functools
import math

import jax
import jax.numpy as jnp
from jax import lax
from jax.experimental import pallas as pl
from jax.experimental.pallas import tpu as pltpu

D_MODEL = 1024
HEAD_DIM = 64
N_HEADS = 8
N_KV_HEADS = 2
GROUP = N_HEADS // N_KV_HEADS
WINDOW = 128
BLOCK = 128
NUM_BUCKETS = 32
MAX_DISTANCE = 128
F_GROUPS = 4
F_CH = 128
ATTN_W = N_HEADS * HEAD_DIM
FOUR_W = F_GROUPS * F_CH
KV_COLS = N_KV_HEADS * HEAD_DIM
QKV_COLS = ATTN_W + 2 * KV_COLS
IN_COLS = QKV_COLS + FOUR_W
D_FF = 2816
EPS = 1e-5
DEPTH = 1
ALPHA = (2.0 * DEPTH) ** 0.25
NEG = -1e30
N_PAIRS = N_HEADS // 2
HEAD_ORDER = tuple(h for p in range(N_PAIRS) for h in (p, GROUP + p))
Q_BLOCKS = 2

ROW_TILE = 512
FF_CHUNK = 256
N_FF_CHUNKS = D_FF // FF_CHUNK
HALO = 16
RADIX = 64
JB = 16
SUB = 8
VMEM_LIMIT = 56 * 1024 * 1024

bf16 = jnp.bfloat16
f32 = jnp.float32


def _layer_norm(x, g, b):
    mu = jnp.mean(x, axis=-1, keepdims=True)
    xc = x - mu
    var = jnp.mean(xc * xc, axis=-1, keepdims=True)
    return xc * lax.rsqrt(var + EPS) * g + b


def _rms_norm(x, g):
    return x * lax.rsqrt(jnp.mean(x * x, axis=-1, keepdims=True) + EPS) * g


def _dot(a, b):
    return jnp.dot(a, b, preferred_element_type=f32)


def _in_proj_kernel(x_ref, g_ref, b_ref, w_ref, dft_ref, qkv_ref, y_ref):
    x = x_ref[0].reshape(RADIX * JB, D_MODEL)
    h = _dot(_layer_norm(x, g_ref[...], b_ref[...]).astype(bf16), w_ref[...])
    qkv_ref[0] = h[:, :QKV_COLS].astype(bf16).reshape(RADIX, JB, QKV_COLS)
    f = h[:, QKV_COLS:].reshape(RADIX, JB, FOUR_W)
    for hh in range(JB // SUB):
        fh = f[:, hh * SUB:(hh + 1) * SUB, :].reshape(RADIX * SUB, FOUR_W).astype(bf16)
        y = _dot(dft_ref[0, hh], fh)
        y_ref[0, :, hh * SUB:(hh + 1) * SUB] = y.reshape(2, SUB, RADIX, FOUR_W)


def _in_proj(x, ln_g, ln_b, w_in, dft1):
    B = x.shape[0]
    x4 = x.reshape(B, RADIX, RADIX, D_MODEL)
    return pl.pallas_call(
        _in_proj_kernel,
        grid=(RADIX // JB, B),
        in_specs=[
            pl.BlockSpec((1, RADIX, JB, D_MODEL), lambda jb, b: (b, 0, jb, 0)),
            pl.BlockSpec((1, D_MODEL), lambda jb, b: (0, 0)),
            pl.BlockSpec((1, D_MODEL), lambda jb, b: (0, 0)),
            pl.BlockSpec((D_MODEL, IN_COLS), lambda jb, b: (0, 0)),
            pl.BlockSpec((1, JB // SUB, 2 * RADIX * SUB, RADIX * SUB), lambda jb, b: (jb, 0, 0, 0)),
        ],
        out_specs=[
            pl.BlockSpec((1, RADIX, JB, QKV_COLS), lambda jb, b: (b, 0, jb, 0)),
            pl.BlockSpec((1, 2, JB, RADIX, FOUR_W), lambda jb, b: (b, 0, jb, 0, 0)),
        ],
        out_shape=[
            jax.ShapeDtypeStruct((B, RADIX, RADIX, QKV_COLS), bf16),
            jax.ShapeDtypeStruct((B, 2, RADIX, RADIX, FOUR_W), f32),
        ],
        compiler_params=pltpu.CompilerParams(
            dimension_semantics=("arbitrary", "arbitrary"), vmem_limit_bytes=VMEM_LIMIT),
        name="in_proj",
    )(x4, ln_g, ln_b, w_in, dft1)


def _attn_kernel(q_ref, k_ref, v_ref, bias_ref, sink_ref, o_ref, *, nblk):
    t = pl.program_id(1)
    seq = nblk * BLOCK
    keys = 3 * BLOCK
    lane = lax.broadcasted_iota(jnp.int32, (keys, KV_COLS), 1)
    low = lane < HEAD_DIM
    out_low = lax.broadcasted_iota(jnp.int32, (BLOCK, KV_COLS), 1) < HEAD_DIM

    def block_diag(ref, starts):
        win = jnp.concatenate([ref[0, pl.ds(s, BLOCK), :] for s in starts], axis=0)
        zero = jnp.zeros_like(win)
        return jnp.concatenate([jnp.where(low, win, zero), jnp.where(low, zero, win)], axis=0)

    def scores(qi):
        i = t * Q_BLOCKS + qi
        starts = [pl.multiple_of(jnp.clip((i + c) * BLOCK, 0, seq - BLOCK), BLOCK) for c in (-1, 0, 1)]
        q = q_ref[0, qi * BLOCK:(qi + 1) * BLOCK, :]
        qs = jnp.concatenate([q[:, p * KV_COLS:(p + 1) * KV_COLS] for p in range(N_PAIRS)], axis=0)
        logits = lax.dot_general(qs, block_diag(k_ref, starts), (((1,), (1,)), ((), ())),
                                 preferred_element_type=f32)
        variant = jnp.where(i == 0, 1, jnp.where(i == nblk - 1, 2, 0))
        return logits + bias_ref[variant], block_diag(v_ref, starts)

    def finish(qi, logits, vbd):
        probs, scales = [], []
        for p in range(N_PAIRS):
            halves, recips = [], []
            for side in range(2):
                l = logits[p * BLOCK:(p + 1) * BLOCK, side * keys:(side + 1) * keys]
                sink = sink_ref[2 * p + side]
                m = jnp.maximum(jnp.max(l, axis=-1, keepdims=True), sink)
                e = jnp.exp(l - m)
                denom = jnp.sum(e, axis=-1, keepdims=True) + jnp.exp(sink - m)
                halves.append(e.astype(bf16))
                recips.append(1.0 / denom)
            probs.append(jnp.concatenate(halves, axis=-1))
            scales.append(jnp.where(out_low, recips[0], recips[1]))
        pv = _dot(jnp.concatenate(probs, axis=0), vbd)
        out = [pv[p * BLOCK:(p + 1) * BLOCK, :] * scales[p] for p in range(N_PAIRS)]
        o_ref[0, qi * BLOCK:(qi + 1) * BLOCK, :] = jnp.concatenate(out, axis=-1).astype(bf16)

    staged = [scores(qi) for qi in range(Q_BLOCKS)]
    for qi in range(Q_BLOCKS):
        finish(qi, *staged[qi])


def _attention(qkv, bias, sink):
    B, S, _ = qkv.shape
    nblk = S // BLOCK
    assert nblk >= 2 and nblk % Q_BLOCKS == 0
    k_col = ATTN_W // KV_COLS
    rows = Q_BLOCKS * BLOCK
    return pl.pallas_call(
        functools.partial(_attn_kernel, nblk=nblk),
        grid=(B, nblk // Q_BLOCKS),
        in_specs=[
            pl.BlockSpec((1, rows, ATTN_W), lambda b, t: (b, t, 0)),
            pl.BlockSpec((1, S, KV_COLS), lambda b, t: (b, 0, k_col)),
            pl.BlockSpec((1, S, KV_COLS), lambda b, t: (b, 0, k_col + 1)),
            pl.BlockSpec((3, N_PAIRS * BLOCK, 6 * BLOCK), lambda b, t: (0, 0, 0)),
            pl.BlockSpec(memory_space=pltpu.SMEM),
        ],
        out_specs=pl.BlockSpec((1, rows, ATTN_W), lambda b, t: (b, t, 0)),
        out_shape=jax.ShapeDtypeStruct((B, S, ATTN_W), bf16),
        compiler_params=pltpu.CompilerParams(
            dimension_semantics=("arbitrary", "arbitrary"), vmem_limit_bytes=VMEM_LIMIT),
        name="attn",
    )(qkv, qkv, qkv, bias, sink)


def _bias_kernel(bucket_ref, rel_ref, table_ref, o_ref):
    bucket = bucket_ref[...]
    in_window = jnp.abs(rel_ref[...]) <= WINDOW
    for h in range(N_HEADS):
        acc = jnp.zeros(bucket.shape, f32)
        for b in range(NUM_BUCKETS):
            acc = jnp.where(bucket == b, table_ref[b, h], acc)
        o_ref[h] = jnp.where(in_window, acc, NEG)


def _band_bias(rel_table):
    rel = (jnp.arange(3 * BLOCK, dtype=jnp.int32)[None, :] - BLOCK) - jnp.arange(BLOCK, dtype=jnp.int32)[:, None]
    bias = pl.pallas_call(
        _bias_kernel,
        in_specs=[pl.BlockSpec(rel.shape, lambda: (0, 0)), pl.BlockSpec(rel.shape, lambda: (0, 0)),
                  pl.BlockSpec(memory_space=pltpu.SMEM)],
        out_specs=pl.BlockSpec((N_HEADS,) + rel.shape, lambda: (0, 0, 0)),
        out_shape=jax.ShapeDtypeStruct((N_HEADS,) + rel.shape, f32),
        name="bias_table",
    )(_t5_bucket(rel), rel, rel_table.astype(f32))
    chunk = jnp.arange(3 * BLOCK, dtype=jnp.int32) // BLOCK
    variants = jnp.stack([bias, jnp.where(chunk == 0, NEG, bias), jnp.where(chunk == 2, NEG, bias)])
    pairs = jnp.stack([variants[:, h] for h in HEAD_ORDER], axis=1)
    pairs = pairs.reshape(3, N_PAIRS, 2, BLOCK, 3 * BLOCK)
    return pairs.transpose(0, 1, 3, 2, 4).reshape(3, N_PAIRS * BLOCK, 6 * BLOCK)


def _four_kernel(y_ref, ha_ref, hp_ref, cc_ref, nsc_ref, wf_ref, o_ref):
    y = y_ref[0]
    halves = []
    for hh in range(JB // SUB):
        ys = y[:, :, hh * SUB:(hh + 1) * SUB, :].reshape(2 * RADIX * SUB, FOUR_W).astype(bf16)
        a = _dot(ha_ref[...], ys).astype(bf16)
        p = _dot(hp_ref[...], ys).astype(bf16)
        four = []
        for g in range(F_GROUPS):
            cols = slice(g * F_CH, (g + 1) * F_CH)
            zc = _dot(a[:, cols], cc_ref[...]) + _dot(p[:, cols], nsc_ref[...])
            four.append(_dot(zc.astype(bf16), wf_ref[g]))
        halves.append(jnp.concatenate(four, axis=-1).reshape(RADIX, SUB, FOUR_W))
    o_ref[0] = jnp.concatenate(halves, axis=1).astype(bf16)


def _fourier(y, ha, hp, cc, nsc, w_f):
    B = y.shape[0]
    full = lambda *shape: pl.BlockSpec(shape, lambda b, kb: (0,) * len(shape))
    return pl.pallas_call(
        _four_kernel,
        grid=(B, RADIX // JB),
        in_specs=[
            pl.BlockSpec((1, 2, RADIX, JB, FOUR_W), lambda b, kb: (b, 0, 0, kb, 0)),
            full(RADIX * SUB, 2 * RADIX * SUB), full(RADIX * SUB, 2 * RADIX * SUB),
            full(F_CH, F_CH), full(F_CH, F_CH), full(F_GROUPS, F_CH, F_CH),
        ],
        out_specs=pl.BlockSpec((1, RADIX, JB, FOUR_W), lambda b, kb: (b, 0, kb, 0)),
        out_shape=jax.ShapeDtypeStruct((B, RADIX, RADIX, FOUR_W), bf16),
        compiler_params=pltpu.CompilerParams(
            dimension_semantics=("arbitrary", "arbitrary"), vmem_limit_bytes=VMEM_LIMIT),
        name="fourier",
    )(y, ha, hp, cc, nsc, w_f)


def _mix_kernel(x_ref, attn_ref, four_ref, ga_ref, gf_ref, wo_ref,
                lg_ref, lb_ref, l1g_ref, l1b_ref, x1_ref):
    x0 = _layer_norm(x_ref[...], lg_ref[...], lb_ref[...])
    an = _rms_norm(attn_ref[...].astype(f32), ga_ref[...]).astype(bf16)
    fn = _rms_norm(four_ref[...].astype(f32), gf_ref[...]).astype(bf16)
    mix = _dot(an, wo_ref[:ATTN_W, :]) + _dot(fn, wo_ref[ATTN_W:, :])
    x1_ref[...] = _layer_norm(ALPHA * x0 + mix, l1g_ref[...], l1b_ref[...])


def _mix(x2d, attn2d, four2d, g_attn, g_four, w_o, ln_g, ln_b, ln1_g, ln1_b):
    T = x2d.shape[0]
    row = lambda w: pl.BlockSpec((ROW_TILE, w), lambda t: (t, 0))
    full = lambda *shape: pl.BlockSpec(shape, lambda t: (0,) * len(shape))
    return pl.pallas_call(
        _mix_kernel,
        grid=(T // ROW_TILE,),
        in_specs=[
            row(D_MODEL), row(ATTN_W), row(FOUR_W),
            full(1, ATTN_W), full(1, FOUR_W), full(D_MODEL, D_MODEL),
            full(1, D_MODEL), full(1, D_MODEL), full(1, D_MODEL), full(1, D_MODEL),
        ],
        out_specs=row(D_MODEL),
        out_shape=jax.ShapeDtypeStruct((T, D_MODEL), f32),
        compiler_params=pltpu.CompilerParams(
            dimension_semantics=("arbitrary",), vmem_limit_bytes=VMEM_LIMIT),
        name="mix",
    )(x2d, attn2d, four2d, g_attn, g_four, w_o, ln_g, ln_b, ln1_g, ln1_b)


def _gelu(x):
    return 0.5 * x * (1.0 + lax.erf(x * (1.0 / math.sqrt(2.0))))


def _ffn_kernel(xp_ref, xm_ref, xn_ref, wup_ref, cw_ref, cb_ref, wd_ref, l2g_ref, l2b_ref,
                y_ref, xs_ref, h_ref, act_ref, *, tiles_per_seq):
    t = pl.program_id(0)
    first = (t % tiles_per_seq) == 0
    last = (t % tiles_per_seq) == tiles_per_seq - 1
    xs_ref[0:HALO, :] = jnp.where(first, 0.0, xp_ref[...]).astype(bf16)
    xs_ref[HALO:HALO + ROW_TILE, :] = xm_ref[...].astype(bf16)
    xs_ref[HALO + ROW_TILE:, :] = jnp.where(last, 0.0, xn_ref[...]).astype(bf16)

    def conv(slot, col):
        cw = cw_ref[:, col:col + FF_CHUNK]
        prev = h_ref[slot, HALO - 1:HALO - 1 + ROW_TILE, :]
        cur = h_ref[slot, HALO:HALO + ROW_TILE, :]
        nxt = h_ref[slot, HALO + 1:HALO + 1 + ROW_TILE, :]
        return (cw[0:1, :] * prev + cw[1:2, :] * cur + cw[2:3, :] * nxt
                + cb_ref[:, col:col + FF_CHUNK])

    for c in range(N_FF_CHUNKS):
        sa, su = 2 * (c % 2), 2 * (c % 2) + 1
        col_a, col_u = c * FF_CHUNK, D_FF + c * FF_CHUNK
        h_ref[sa] = _dot(xs_ref[...], wup_ref[:, col_a:col_a + FF_CHUNK])
        h_ref[su] = _dot(xs_ref[...], wup_ref[:, col_u:col_u + FF_CHUNK])
        act = _gelu(conv(sa, col_a)) * conv(su, col_u)
        act_ref[:, col_a:col_a + FF_CHUNK] = act.astype(bf16)
    ffn = _dot(act_ref[...], wd_ref[...])
    y_ref[...] = _layer_norm(ALPHA * xm_ref[...] + ffn, l2g_ref[...], l2b_ref[...])


def _ffn(x1, seq_len, w_up, conv_w, conv_b, w_down, ln2_g, ln2_b):
    T = x1.shape[0]
    tiles_per_seq = seq_len // ROW_TILE
    hb = ROW_TILE // HALO
    n_hb = T // HALO
    full = lambda *shape: pl.BlockSpec(shape, lambda t: (0,) * len(shape),
                                       pipeline_mode=pl.Buffered(1))
    m = ROW_TILE + 2 * HALO
    return pl.pallas_call(
        functools.partial(_ffn_kernel, tiles_per_seq=tiles_per_seq),
        grid=(T // ROW_TILE,),
        in_specs=[
            pl.BlockSpec((HALO, D_MODEL), lambda t: (jnp.maximum(t * hb - 1, 0), 0)),
            pl.BlockSpec((ROW_TILE, D_MODEL), lambda t: (t, 0)),
            pl.BlockSpec((HALO, D_MODEL), lambda t: (jnp.minimum((t + 1) * hb, n_hb - 1), 0)),
            full(D_MODEL, 2 * D_FF), full(3, 2 * D_FF), full(1, 2 * D_FF),
            full(D_FF, D_MODEL), full(1, D_MODEL), full(1, D_MODEL),
        ],
        out_specs=pl.BlockSpec((ROW_TILE, D_MODEL), lambda t: (t, 0)),
        out_shape=jax.ShapeDtypeStruct((T, D_MODEL), f32),
        scratch_shapes=[
            pltpu.VMEM((m, D_MODEL), bf16),
            pltpu.VMEM((4, m, FF_CHUNK), f32),
            pltpu.VMEM((ROW_TILE, D_FF), bf16),
        ],
        compiler_params=pltpu.CompilerParams(
            dimension_semantics=("arbitrary",), vmem_limit_bytes=VMEM_LIMIT),
        name="ffn",
    )(x1, x1, x1, w_up, conv_w, conv_b, w_down, ln2_g, ln2_b)


def _t5_bucket(rel):
    nb = NUM_BUCKETS // 2
    ret = jnp.where(rel > 0, nb, 0)
    n = jnp.abs(rel)
    max_exact = nb // 2
    nf = jnp.maximum(n, 1).astype(f32)
    large = max_exact + (jnp.log(nf / max_exact) / math.log(MAX_DISTANCE / max_exact)
                         * (nb - max_exact)).astype(jnp.int32)
    large = jnp.minimum(large, nb - 1)
    return ret + jnp.where(n < max_exact, n, large)


def _trig(num, den):
    ang = (num % den).astype(f32) * (2.0 * math.pi / den)
    return jnp.cos(ang), jnp.sin(ang)


def _stage1_tables(seq):
    ar = lambda n: jnp.arange(n, dtype=jnp.int32)
    j = (ar(RADIX // JB)[:, None, None] * JB + ar(JB // SUB)[None, :, None] * SUB + ar(SUB)[None, None, :])
    n = RADIX * ar(RADIX)[None, None, None, None, :] + j[:, :, :, None, None]
    c, s = _trig(ar(RADIX)[None, None, None, :, None] * n, seq)
    val = jnp.stack([c, s], axis=2) * (seq ** -0.5)
    g = val[..., None] * jnp.eye(SUB, dtype=f32)[None, None, None, :, None, None, :]
    return g.reshape(RADIX // JB, JB // SUB, 2 * SUB * RADIX, RADIX * SUB).astype(bf16)


def _stage2_tables():
    ar = jnp.arange(RADIX, dtype=jnp.int32)
    c, s = _trig(ar[:, None] * ar[None, :], RADIX)
    eye = jnp.eye(SUB, dtype=f32)

    def expand(parts):
        m = jnp.stack(parts)[:, :, :, None, None] * eye[None, None, None]
        return m.transpose(1, 3, 0, 2, 4).reshape(RADIX * SUB, 2 * RADIX * SUB).astype(bf16)

    return expand([c, -s]), expand([s, c])


def _encode(x, p):
    B, S, _ = x.shape
    assert S == RADIX * RADIX
    T = B * S
    x2d = x.reshape(T, D_MODEL)
    qkv, y1 = _in_proj(x, p["ln_in_g"], p["ln_in_b"], p["w_in"], p["dft1"])
    attn = _attention(qkv.reshape(B, S, QKV_COLS), p["bias"], p["sink"])
    four = _fourier(y1, p["ha"], p["hp"], p["cc"], p["nsc"], p["w_f"])
    x1 = _mix(x2d, attn.reshape(T, ATTN_W), four.reshape(T, FOUR_W),
              p["g_attn"], p["g_four"], p["w_o"], p["ln_in_g"], p["ln_in_b"], p["ln1_g"], p["ln1_b"])
    y = _ffn(x1, S, p["w_up"], p["conv_w"], p["conv_b"], p["w_down"], p["ln2_g"], p["ln2_b"])
    return y.reshape(B, S, D_MODEL)


def kernel(x_prompt, x_sample, ln_in_g, ln_in_b, rel_table, w_in, attn_sink, w_fourier, g_attn,
           g_fourier, w_o, ln1_g, ln1_b, w_up, conv_w, conv_b, w_down, ln2_g, ln2_b):
    S = x_prompt.shape[1]
    row = lambda v: v.reshape(1, -1).astype(f32)
    col_scale = jnp.concatenate([jnp.full((ATTN_W,), HEAD_DIM ** -0.5, f32),
                                 jnp.ones((IN_COLS - ATTN_W,), f32)])
    ar = jnp.arange(F_CH, dtype=jnp.int32)
    cc, sc = _trig(ar[:, None] * ar[None, :], F_CH)
    ha, hp = _stage2_tables()

    def by_pairs(w, axis):
        heads = jnp.split(w, N_HEADS, axis=axis)
        return jnp.concatenate([heads[h] for h in HEAD_ORDER], axis=axis)

    w_in_s = w_in[0] * col_scale
    w_in_p = jnp.concatenate([by_pairs(w_in_s[:, :ATTN_W], 1), w_in_s[:, ATTN_W:]], axis=1)
    w_o_p = jnp.concatenate([by_pairs(w_o[0][:ATTN_W], 0), w_o[0][ATTN_W:]], axis=0)

    p = dict(
        ln_in_g=row(ln_in_g), ln_in_b=row(ln_in_b),
        w_in=w_in_p.astype(bf16),
        bias=_band_bias(rel_table), sink=by_pairs(attn_sink[0].astype(f32), 0),
        dft1=_stage1_tables(S), ha=ha, hp=hp,
        cc=(cc * F_CH ** -0.5).astype(bf16), nsc=(-sc * F_CH ** -0.5).astype(bf16),
        w_f=w_fourier[0].astype(bf16),
        g_attn=row(by_pairs(g_attn[0], 0)), g_four=row(g_fourier[0]),
        w_o=w_o_p.astype(bf16),
        ln1_g=row(ln1_g[0]), ln1_b=row(ln1_b[0]),
        w_up=w_up[0].astype(bf16), conv_w=conv_w[0].astype(f32), conv_b=row(conv_b[0]),
        w_down=w_down[0].astype(bf16),
        ln2_g=row(ln2_g[0]), ln2_b=row(ln2_b[0]),
    )
    return (_encode(x_prompt, p), _encode(x_sample, p))
```

```python
import functools
import math

import jax
import jax.numpy as jnp
from jax import lax
from jax.experimental import pallas as pl
from jax.experimental.pallas import tpu as pltpu

D_MODEL = 1024
HEAD_DIM = 64
N_HEADS = 8
N_KV_HEADS = 2
GROUP = N_HEADS // N_KV_HEADS
WINDOW = 128
BLOCK = 128
NUM_BUCKETS = 32
MAX_DISTANCE = 128
F_GROUPS = 4
F_CH = 128
ATTN_W = N_HEADS * HEAD_DIM
FOUR_W = F_GROUPS * F_CH
KV_COLS = N_KV_HEADS * HEAD_DIM
QKV_COLS = ATTN_W + 2 * KV_COLS
IN_COLS = QKV_COLS + FOUR_W
D_FF = 2816
EPS = 1e-5
DEPTH = 1
ALPHA = (2.0 * DEPTH) ** 0.25
NEG = -1e30
N_PAIRS = N_HEADS // 2
HEAD_ORDER = tuple(h for p in range(N_PAIRS) for h in (p, GROUP + p))
Q_BLOCKS = 2

ROW_TILE = 512
FF_CHUNK = 256
N_FF_CHUNKS = D_FF // FF_CHUNK
HALO = 16
RADIX = 64
JB = 16
SUB = 8
VMEM_LIMIT = 56 * 1024 * 1024

bf16 = jnp.bfloat16
f32 = jnp.float32


def _layer_norm(x, g, b):
    mu = jnp.mean(x, axis=-1, keepdims=True)
    xc = x - mu
    var = jnp.mean(xc * xc, axis=-1, keepdims=True)
    return xc * lax.rsqrt(var + EPS) * g + b


def _rms_norm(x, g):
    return x * lax.rsqrt(jnp.mean(x * x, axis=-1, keepdims=True) + EPS) * g


def _dot(a, b):
    return jnp.dot(a, b, preferred_element_type=f32)


def _in_proj_kernel(x_ref, g_ref, b_ref, w_ref, dft_ref, qkv_ref, y_ref, res_ref):
    x0 = _layer_norm(x_ref[0].reshape(RADIX * JB, D_MODEL), g_ref[...], b_ref[...])
    res_ref[0] = (ALPHA * x0).reshape(RADIX, JB, D_MODEL)
    h = _dot(x0.astype(bf16), w_ref[...])
    qkv_ref[0] = h[:, :QKV_COLS].astype(bf16).reshape(RADIX, JB, QKV_COLS)
    f = h[:, QKV_COLS:].reshape(RADIX, JB, FOUR_W)
    for hh in range(JB // SUB):
        fh = f[:, hh * SUB:(hh + 1) * SUB, :].reshape(RADIX * SUB, FOUR_W).astype(bf16)
        y = _dot(dft_ref[0, hh], fh)
        y_ref[0, :, hh * SUB:(hh + 1) * SUB] = y.reshape(2, SUB, RADIX, FOUR_W)


def _in_proj(x, ln_g, ln_b, w_in, dft1):
    B = x.shape[0]
    x4 = x.reshape(B, RADIX, RADIX, D_MODEL)
    return pl.pallas_call(
        _in_proj_kernel,
        grid=(RADIX // JB, B),
        in_specs=[
            pl.BlockSpec((1, RADIX, JB, D_MODEL), lambda jb, b: (b, 0, jb, 0)),
            pl.BlockSpec((1, D_MODEL), lambda jb, b: (0, 0)),
            pl.BlockSpec((1, D_MODEL), lambda jb, b: (0, 0)),
            pl.BlockSpec((D_MODEL, IN_COLS), lambda jb, b: (0, 0)),
            pl.BlockSpec((1, JB // SUB, 2 * RADIX * SUB, RADIX * SUB), lambda jb, b: (jb, 0, 0, 0)),
        ],
        out_specs=[
            pl.BlockSpec((1, RADIX, JB, QKV_COLS), lambda jb, b: (b, 0, jb, 0)),
            pl.BlockSpec((1, 2, JB, RADIX, FOUR_W), lambda jb, b: (b, 0, jb, 0, 0)),
            pl.BlockSpec((1, RADIX, JB, D_MODEL), lambda jb, b: (b, 0, jb, 0)),
        ],
        out_shape=[
            jax.ShapeDtypeStruct((B, RADIX, RADIX, QKV_COLS), bf16),
            jax.ShapeDtypeStruct((B, 2, RADIX, RADIX, FOUR_W), f32),
            jax.ShapeDtypeStruct((B, RADIX, RADIX, D_MODEL), f32),
        ],
        compiler_params=pltpu.CompilerParams(
            dimension_semantics=("arbitrary", "arbitrary"), vmem_limit_bytes=VMEM_LIMIT),
        name="in_proj",
    )(x4, ln_g, ln_b, w_in, dft1)


def _attn_kernel(q_ref, k_ref, v_ref, bias_ref, sink_ref, o_ref, *, nblk):
    t = pl.program_id(1)
    seq = nblk * BLOCK
    keys = 3 * BLOCK
    lane = lax.broadcasted_iota(jnp.int32, (keys, KV_COLS), 1)
    low = lane < HEAD_DIM
    out_low = lax.broadcasted_iota(jnp.int32, (BLOCK, KV_COLS), 1) < HEAD_DIM

    def block_diag(ref, starts):
        win = jnp.concatenate([ref[0, pl.ds(s, BLOCK), :] for s in starts], axis=0)
        zero = jnp.zeros_like(win)
        return jnp.concatenate([jnp.where(low, win, zero), jnp.where(low, zero, win)], axis=0)

    def scores(qi):
        i = t * Q_BLOCKS + qi
        starts = [pl.multiple_of(jnp.clip((i + c) * BLOCK, 0, seq - BLOCK), BLOCK) for c in (-1, 0, 1)]
        q = q_ref[0, qi * BLOCK:(qi + 1) * BLOCK, :]
        qs = jnp.concatenate([q[:, p * KV_COLS:(p + 1) * KV_COLS] for p in range(N_PAIRS)], axis=0)
        logits = lax.dot_general(qs, block_diag(k_ref, starts), (((1,), (1,)), ((), ())),
                                 preferred_element_type=f32)
        variant = jnp.where(i == 0, 1, jnp.where(i == nblk - 1, 2, 0))
        return logits + bias_ref[variant], block_diag(v_ref, starts)

    def finish(qi, logits, vbd):
        probs, scales = [], []
        for p in range(N_PAIRS):
            halves, recips = [], []
            for side in range(2):
                l = logits[p * BLOCK:(p + 1) * BLOCK, side * keys:(side + 1) * keys]
                sink = sink_ref[2 * p + side]
                m = jnp.maximum(jnp.max(l, axis=-1, keepdims=True), sink)
                e = jnp.exp(l - m)
                denom = jnp.sum(e, axis=-1, keepdims=True) + jnp.exp(sink - m)
                halves.append(e.astype(bf16))
                recips.append(1.0 / denom)
            probs.append(jnp.concatenate(halves, axis=-1))
            scales.append(jnp.where(out_low, recips[0], recips[1]))
        pv = _dot(jnp.concatenate(probs, axis=0), vbd)
        out = [pv[p * BLOCK:(p + 1) * BLOCK, :] * scales[p] for p in range(N_PAIRS)]
        o_ref[0, qi * BLOCK:(qi + 1) * BLOCK, :] = jnp.concatenate(out, axis=-1).astype(bf16)

    staged = [scores(qi) for qi in range(Q_BLOCKS)]
    for qi in range(Q_BLOCKS):
        finish(qi, *staged[qi])


def _attention(qkv, bias, sink):
    B, S, _ = qkv.shape
    nblk = S // BLOCK
    assert nblk >= 2 and nblk % Q_BLOCKS == 0
    k_col = ATTN_W // KV_COLS
    rows = Q_BLOCKS * BLOCK
    return pl.pallas_call(
        functools.partial(_attn_kernel, nblk=nblk),
        grid=(B, nblk // Q_BLOCKS),
        in_specs=[
            pl.BlockSpec((1, rows, ATTN_W), lambda b, t: (b, t, 0)),
            pl.BlockSpec((1, S, KV_COLS), lambda b, t: (b, 0, k_col)),
            pl.BlockSpec((1, S, KV_COLS), lambda b, t: (b, 0, k_col + 1)),
            pl.BlockSpec((3, N_PAIRS * BLOCK, 6 * BLOCK), lambda b, t: (0, 0, 0)),
            pl.BlockSpec(memory_space=pltpu.SMEM),
        ],
        out_specs=pl.BlockSpec((1, rows, ATTN_W), lambda b, t: (b, t, 0)),
        out_shape=jax.ShapeDtypeStruct((B, S, ATTN_W), bf16),
        compiler_params=pltpu.CompilerParams(
            dimension_semantics=("arbitrary", "arbitrary"), vmem_limit_bytes=VMEM_LIMIT),
        name="attn",
    )(qkv, qkv, qkv, bias, sink)


def _bias_kernel(bucket_ref, rel_ref, table_ref, o_ref):
    bucket = bucket_ref[...]
    in_window = jnp.abs(rel_ref[...]) <= WINDOW
    for h in range(N_HEADS):
        acc = jnp.zeros(bucket.shape, f32)
        for b in range(NUM_BUCKETS):
            acc = jnp.where(bucket == b, table_ref[b, h], acc)
        o_ref[h] = jnp.where(in_window, acc, NEG)


def _band_bias(rel_table):
    rel = (jnp.arange(3 * BLOCK, dtype=jnp.int32)[None, :] - BLOCK) - jnp.arange(BLOCK, dtype=jnp.int32)[:, None]
    bias = pl.pallas_call(
        _bias_kernel,
        in_specs=[pl.BlockSpec(rel.shape, lambda: (0, 0)), pl.BlockSpec(rel.shape, lambda: (0, 0)),
                  pl.BlockSpec(memory_space=pltpu.SMEM)],
        out_specs=pl.BlockSpec((N_HEADS,) + rel.shape, lambda: (0, 0, 0)),
        out_shape=jax.ShapeDtypeStruct((N_HEADS,) + rel.shape, f32),
        name="bias_table",
    )(_t5_bucket(rel), rel, rel_table.astype(f32))
    chunk = jnp.arange(3 * BLOCK, dtype=jnp.int32) // BLOCK
    variants = jnp.stack([bias, jnp.where(chunk == 0, NEG, bias), jnp.where(chunk == 2, NEG, bias)])
    pairs = jnp.stack([variants[:, h] for h in HEAD_ORDER], axis=1)
    pairs = pairs.reshape(3, N_PAIRS, 2, BLOCK, 3 * BLOCK)
    return pairs.transpose(0, 1, 3, 2, 4).reshape(3, N_PAIRS * BLOCK, 6 * BLOCK)


def _four_kernel(y_ref, ha_ref, hp_ref, cw_ref, o_ref):
    y = y_ref[0]
    halves = []
    for hh in range(JB // SUB):
        ys = y[:, :, hh * SUB:(hh + 1) * SUB, :].reshape(2 * RADIX * SUB, FOUR_W).astype(bf16)
        a = _dot(ha_ref[...], ys).astype(bf16)
        p = _dot(hp_ref[...], ys).astype(bf16)
        four = []
        for g in range(F_GROUPS):
            cols = slice(g * F_CH, (g + 1) * F_CH)
            ap = jnp.concatenate([a[:, cols], p[:, cols]], axis=-1)
            four.append(_dot(ap, cw_ref[g]))
        halves.append(jnp.concatenate(four, axis=-1).reshape(RADIX, SUB, FOUR_W))
    o_ref[0] = jnp.concatenate(halves, axis=1).astype(bf16)


def _fourier(y, ha, hp, cw):
    B = y.shape[0]
    full = lambda *shape: pl.BlockSpec(shape, lambda b, kb: (0,) * len(shape))
    return pl.pallas_call(
        _four_kernel,
        grid=(B, RADIX // JB),
        in_specs=[
            pl.BlockSpec((1, 2, RADIX, JB, FOUR_W), lambda b, kb: (b, 0, 0, kb, 0)),
            full(RADIX * SUB, 2 * RADIX * SUB), full(RADIX * SUB, 2 * RADIX * SUB),
            full(F_GROUPS, 2 * F_CH, F_CH),
        ],
        out_specs=pl.BlockSpec((1, RADIX, JB, FOUR_W), lambda b, kb: (b, 0, kb, 0)),
        out_shape=jax.ShapeDtypeStruct((B, RADIX, RADIX, FOUR_W), bf16),
        compiler_params=pltpu.CompilerParams(
            dimension_semantics=("arbitrary", "arbitrary"), vmem_limit_bytes=VMEM_LIMIT),
        name="fourier",
    )(y, ha, hp, cw)


def _fold_kernel(cs_ref, wf_ref, o_ref):
    for g in range(F_GROUPS):
        o_ref[g] = jnp.dot(cs_ref[...], wf_ref[g], preferred_element_type=f32,
                           precision=lax.Precision.HIGHEST).astype(bf16)


def _fold_channel_dft(w_f):
    ar = jnp.arange(F_CH, dtype=jnp.int32)
    cc, sc = _trig(ar[:, None] * ar[None, :], F_CH)
    cs = jnp.concatenate([cc, -sc], axis=0) * F_CH ** -0.5
    return pl.pallas_call(
        _fold_kernel,
        out_shape=jax.ShapeDtypeStruct((F_GROUPS, 2 * F_CH, F_CH), bf16),
        name="fold_channel_dft",
    )(cs, w_f.astype(f32))


def _mix_kernel(res_ref, attn_ref, four_ref, ga_ref, gf_ref, wo_ref, l1g_ref, l1b_ref, x1_ref):
    an = _rms_norm(attn_ref[...].astype(f32), ga_ref[...]).astype(bf16)
    fn = _rms_norm(four_ref[...].astype(f32), gf_ref[...]).astype(bf16)
    mix = _dot(an, wo_ref[:ATTN_W, :]) + _dot(fn, wo_ref[ATTN_W:, :])
    x1_ref[...] = _layer_norm(res_ref[...] + mix, l1g_ref[...], l1b_ref[...])


def _mix(res2d, attn2d, four2d, g_attn, g_four, w_o, ln1_g, ln1_b):
    T = res2d.shape[0]
    row = lambda w: pl.BlockSpec((ROW_TILE, w), lambda t: (t, 0))
    full = lambda *shape: pl.BlockSpec(shape, lambda t: (0,) * len(shape))
    return pl.pallas_call(
        _mix_kernel,
        grid=(T // ROW_TILE,),
        in_specs=[
            row(D_MODEL), row(ATTN_W), row(FOUR_W),
            full(1, ATTN_W), full(1, FOUR_W), full(D_MODEL, D_MODEL),
            full(1, D_MODEL), full(1, D_MODEL),
        ],
        out_specs=row(D_MODEL),
        out_shape=jax.ShapeDtypeStruct((T, D_MODEL), f32),
        compiler_params=pltpu.CompilerParams(
            dimension_semantics=("arbitrary",), vmem_limit_bytes=VMEM_LIMIT),
        name="mix",
    )(res2d, attn2d, four2d, g_attn, g_four, w_o, ln1_g, ln1_b)


def _gelu(x):
    return 0.5 * x * (1.0 + lax.erf(x * (1.0 / math.sqrt(2.0))))


def _ffn_kernel(xp_ref, xm_ref, xn_ref, wup_ref, cw_ref, cb_ref, wd_ref, l2g_ref, l2b_ref,
                y_ref, xs_ref, h_ref, act_ref, *, tiles_per_seq):
    t = pl.program_id(0)
    first = (t % tiles_per_seq) == 0
    last = (t % tiles_per_seq) == tiles_per_seq - 1
    xs_ref[0:HALO, :] = jnp.where(first, 0.0, xp_ref[...]).astype(bf16)
    xs_ref[HALO:HALO + ROW_TILE, :] = xm_ref[...].astype(bf16)
    xs_ref[HALO + ROW_TILE:, :] = jnp.where(last, 0.0, xn_ref[...]).astype(bf16)

    def conv(slot, col):
        cw = cw_ref[:, col:col + FF_CHUNK]
        prev = h_ref[slot, HALO - 1:HALO - 1 + ROW_TILE, :]
        cur = h_ref[slot, HALO:HALO + ROW_TILE, :]
        nxt = h_ref[slot, HALO + 1:HALO + 1 + ROW_TILE, :]
        return (cw[0:1, :] * prev + cw[1:2, :] * cur + cw[2:3, :] * nxt
                + cb_ref[:, col:col + FF_CHUNK])

    for c in range(N_FF_CHUNKS):
        sa, su = 2 * (c % 2), 2 * (c % 2) + 1
        col_a, col_u = c * FF_CHUNK, D_FF + c * FF_CHUNK
        h_ref[sa] = _dot(xs_ref[...], wup_ref[:, col_a:col_a + FF_CHUNK])
        h_ref[su] = _dot(xs_ref[...], wup_ref[:, col_u:col_u + FF_CHUNK])
        act = _gelu(conv(sa, col_a)) * conv(su, col_u)
        act_ref[:, col_a:col_a + FF_CHUNK] = act.astype(bf16)
    for rows in (slice(0, ROW_TILE // 2), slice(ROW_TILE // 2, ROW_TILE)):
        ffn = _dot(act_ref[rows, :], wd_ref[...])
        y_ref[rows, :] = _layer_norm(ALPHA * xm_ref[rows, :] + ffn, l2g_ref[...], l2b_ref[...])


def _ffn(x1, seq_len, w_up, conv_w, conv_b, w_down, ln2_g, ln2_b):
    T = x1.shape[0]
    tiles_per_seq = seq_len // ROW_TILE
    hb = ROW_TILE // HALO
    n_hb = T // HALO
    full = lambda *shape: pl.BlockSpec(shape, lambda t: (0,) * len(shape),
                                       pipeline_mode=pl.Buffered(1))
    m = ROW_TILE + 2 * HALO
    return pl.pallas_call(
        functools.partial(_ffn_kernel, tiles_per_seq=tiles_per_seq),
        grid=(T // ROW_TILE,),
        in_specs=[
            pl.BlockSpec((HALO, D_MODEL), lambda t: (jnp.maximum(t * hb - 1, 0), 0)),
            pl.BlockSpec((ROW_TILE, D_MODEL), lambda t: (t, 0)),
            pl.BlockSpec((HALO, D_MODEL), lambda t: (jnp.minimum((t + 1) * hb, n_hb - 1), 0)),
            full(D_MODEL, 2 * D_FF), full(3, 2 * D_FF), full(1, 2 * D_FF),
            full(D_FF, D_MODEL), full(1, D_MODEL), full(1, D_MODEL),
        ],
        out_specs=pl.BlockSpec((ROW_TILE, D_MODEL), lambda t: (t, 0)),
        out_shape=jax.ShapeDtypeStruct((T, D_MODEL), f32),
        scratch_shapes=[
            pltpu.VMEM((m, D_MODEL), bf16),
            pltpu.VMEM((4, m, FF_CHUNK), f32),
            pltpu.VMEM((ROW_TILE, D_FF), bf16),
        ],
        compiler_params=pltpu.CompilerParams(
            dimension_semantics=("arbitrary",), vmem_limit_bytes=VMEM_LIMIT),
        name="ffn",
    )(x1, x1, x1, w_up, conv_w, conv_b, w_down, ln2_g, ln2_b)


def _t5_bucket(rel):
    nb = NUM_BUCKETS // 2
    ret = jnp.where(rel > 0, nb, 0)
    n = jnp.abs(rel)
    max_exact = nb // 2
    nf = jnp.maximum(n, 1).astype(f32)
    large = max_exact + (jnp.log(nf / max_exact) / math.log(MAX_DISTANCE / max_exact)
                         * (nb - max_exact)).astype(jnp.int32)
    large = jnp.minimum(large, nb - 1)
    return ret + jnp.where(n < max_exact, n, large)


def _trig(num, den):
    ang = (num % den).astype(f32) * (2.0 * math.pi / den)
    return jnp.cos(ang), jnp.sin(ang)


def _stage1_tables(seq):
    shape = (RADIX // JB, JB // SUB, 2 * SUB * RADIX, RADIX * SUB)
    jb, hh, r, c = (lax.broadcasted_iota(jnp.int32, shape, d) for d in range(4))
    part, jj, k1 = r // (SUB * RADIX), (r // RADIX) % SUB, r % RADIX
    i, jj_in = c // SUB, c % SUB
    cos, sin = _trig(k1 * (RADIX * i + jb * JB + hh * SUB + jj), seq)
    val = jnp.where(part == 0, cos, sin) * (seq ** -0.5)
    return jnp.where(jj == jj_in, val, 0.0).astype(bf16)


def _stage2_tables():
    shape = (RADIX * SUB, 2 * RADIX * SUB)
    r, c = (lax.broadcasted_iota(jnp.int32, shape, d) for d in range(2))
    k2, k1 = r // SUB, r % SUB
    part, j, k1_in = c // (RADIX * SUB), (c // SUB) % RADIX, c % SUB
    cos, sin = _trig(j * k2, RADIX)
    same = k1 == k1_in
    ha = jnp.where(same, jnp.where(part == 0, cos, -sin), 0.0)
    hp = jnp.where(same, jnp.where(part == 0, sin, cos), 0.0)
    return ha.astype(bf16), hp.astype(bf16)


def _encode(x, p):
    B, S, _ = x.shape
    assert S == RADIX * RADIX
    T = B * S
    qkv, y1, res = _in_proj(x, p["ln_in_g"], p["ln_in_b"], p["w_in"], p["dft1"])
    attn = _attention(qkv.reshape(B, S, QKV_COLS), p["bias"], p["sink"])
    four = _fourier(y1, p["ha"], p["hp"], p["cw"])
    x1 = _mix(res.reshape(T, D_MODEL), attn.reshape(T, ATTN_W), four.reshape(T, FOUR_W),
              p["g_attn"], p["g_four"], p["w_o"], p["ln1_g"], p["ln1_b"])
    y = _ffn(x1, S, p["w_up"], p["conv_w"], p["conv_b"], p["w_down"], p["ln2_g"], p["ln2_b"])
    return y.reshape(B, S, D_MODEL)


def kernel(x_prompt, x_sample, ln_in_g, ln_in_b, rel_table, w_in, attn_sink, w_fourier, g_attn,
           g_fourier, w_o, ln1_g, ln1_b, w_up, conv_w, conv_b, w_down, ln2_g, ln2_b):
    S = x_prompt.shape[1]
    row = lambda v: v.reshape(1, -1).astype(f32)
    col_scale = jnp.concatenate([jnp.full((ATTN_W,), HEAD_DIM ** -0.5, f32),
                                 jnp.ones((IN_COLS - ATTN_W,), f32)])
    ha, hp = _stage2_tables()

    def by_pairs(w, axis):
        heads = jnp.split(w, N_HEADS, axis=axis)
        return jnp.concatenate([heads[h] for h in HEAD_ORDER], axis=axis)

    w_in_s = w_in[0] * col_scale
    w_in_p = jnp.concatenate([by_pairs(w_in_s[:, :ATTN_W], 1), w_in_s[:, ATTN_W:]], axis=1)
    w_o_p = jnp.concatenate([by_pairs(w_o[0][:ATTN_W], 0), w_o[0][ATTN_W:]], axis=0)

    p = dict(
        ln_in_g=row(ln_in_g), ln_in_b=row(ln_in_b),
        w_in=w_in_p.astype(bf16),
        bias=_band_bias(rel_table), sink=by_pairs(attn_sink[0].astype(f32), 0),
        dft1=_stage1_tables(S), ha=ha, hp=hp,
        cw=_fold_channel_dft(w_fourier[0]),
        g_attn=row(by_pairs(g_attn[0], 0)), g_four=row(g_fourier[0]),
        w_o=w_o_p.astype(bf16),
        ln1_g=row(ln1_g[0]), ln1_b=row(ln1_b[0]),
        w_up=w_up[0].astype(bf16), conv_w=conv_w[0].astype(f32), conv_b=row(conv_b[0]),
        w_down=w_down[0].astype(bf16),
        ln2_g=row(ln2_g[0]), ln2_b=row(ln2_b[0]),
    )
    return (_encode(x_prompt, p), _encode(x_sample, p))
```

```python
import functools
import math

import jax
import jax.numpy as jnp
from jax import lax
from jax.experimental import pallas as pl
from jax.experimental.pallas import tpu as pltpu

D_MODEL = 1024
HEAD_DIM = 64
N_HEADS = 8
N_KV_HEADS = 2
GROUP = N_HEADS // N_KV_HEADS
WINDOW = 128
BLOCK = 128
NUM_BUCKETS = 32
MAX_DISTANCE = 128
F_GROUPS = 4
F_CH = 128
ATTN_W = N_HEADS * HEAD_DIM
FOUR_W = F_GROUPS * F_CH
KV_COLS = N_KV_HEADS * HEAD_DIM
QKV_COLS = ATTN_W + 2 * KV_COLS
IN_COLS = QKV_COLS + FOUR_W
D_FF = 2816
EPS = 1e-5
DEPTH = 1
ALPHA = (2.0 * DEPTH) ** 0.25
NEG = -1e30
LOG2E = math.log2(math.e)
N_PAIRS = N_HEADS // 2
HEAD_ORDER = tuple(h for p in range(N_PAIRS) for h in (p, GROUP + p))
Q_BLOCKS = 2

ROW_TILE = 512
FF_CHUNK = 256
N_FF_CHUNKS = D_FF // FF_CHUNK
HALO = 16
RADIX = 64
JB = 16
SUB = 8
VMEM_LIMIT = 56 * 1024 * 1024

bf16 = jnp.bfloat16
f32 = jnp.float32


def _layer_norm(x, g, b):
    mu = jnp.mean(x, axis=-1, keepdims=True)
    xc = x - mu
    var = jnp.mean(xc * xc, axis=-1, keepdims=True)
    return xc * lax.rsqrt(var + EPS) * g + b


def _rms_norm(x, g):
    return x * lax.rsqrt(jnp.mean(x * x, axis=-1, keepdims=True) + EPS) * g


def _dot(a, b):
    return jnp.dot(a, b, preferred_element_type=f32)


def _in_proj_kernel(x_ref, g_ref, b_ref, w_ref, dft_ref, qkv_ref, y_ref, res_ref):
    qkv = []
    for hh in range(JB // SUB):
        jj = slice(hh * SUB, (hh + 1) * SUB)
        x0 = _layer_norm(x_ref[0, :, jj, :].reshape(RADIX * SUB, D_MODEL), g_ref[...], b_ref[...])
        res_ref[0, :, jj, :] = (ALPHA * x0).reshape(RADIX, SUB, D_MODEL)
        h = _dot(x0.astype(bf16), w_ref[...])
        qkv.append(h[:, :QKV_COLS].reshape(RADIX, SUB, QKV_COLS))
        y = _dot(dft_ref[0, hh], h[:, QKV_COLS:].astype(bf16))
        y_ref[0, :, jj] = y.reshape(2, SUB, RADIX, FOUR_W)
    qkv_ref[0] = jnp.concatenate(qkv, axis=1).astype(bf16)


def _in_proj(x, ln_g, ln_b, w_in, dft1):
    B = x.shape[0]
    x4 = x.reshape(B, RADIX, RADIX, D_MODEL)
    return pl.pallas_call(
        _in_proj_kernel,
        grid=(RADIX // JB, B),
        in_specs=[
            pl.BlockSpec((1, RADIX, JB, D_MODEL), lambda jb, b: (b, 0, jb, 0)),
            pl.BlockSpec((1, D_MODEL), lambda jb, b: (0, 0)),
            pl.BlockSpec((1, D_MODEL), lambda jb, b: (0, 0)),
            pl.BlockSpec((D_MODEL, IN_COLS), lambda jb, b: (0, 0)),
            pl.BlockSpec((1, JB // SUB, 2 * RADIX * SUB, RADIX * SUB), lambda jb, b: (jb, 0, 0, 0)),
        ],
        out_specs=[
            pl.BlockSpec((1, RADIX, JB, QKV_COLS), lambda jb, b: (b, 0, jb, 0)),
            pl.BlockSpec((1, 2, JB, RADIX, FOUR_W), lambda jb, b: (b, 0, jb, 0, 0)),
            pl.BlockSpec((1, RADIX, JB, D_MODEL), lambda jb, b: (b, 0, jb, 0)),
        ],
        out_shape=[
            jax.ShapeDtypeStruct((B, RADIX, RADIX, QKV_COLS), bf16),
            jax.ShapeDtypeStruct((B, 2, RADIX, RADIX, FOUR_W), f32),
            jax.ShapeDtypeStruct((B, RADIX, RADIX, D_MODEL), f32),
        ],
        compiler_params=pltpu.CompilerParams(
            dimension_semantics=("arbitrary", "arbitrary"), vmem_limit_bytes=VMEM_LIMIT),
        name="in_proj",
    )(x4, ln_g, ln_b, w_in, dft1)


def _attn_kernel(q_ref, k_ref, v_ref, bias_ref, sink_ref, o_ref, *, nblk):
    t = pl.program_id(1)
    seq = nblk * BLOCK
    keys = 3 * BLOCK
    lane = lax.broadcasted_iota(jnp.int32, (keys, KV_COLS), 1)
    low = lane < HEAD_DIM
    out_low = lax.broadcasted_iota(jnp.int32, (BLOCK, KV_COLS), 1) < HEAD_DIM

    def block_diag(ref, starts):
        win = jnp.concatenate([ref[0, pl.ds(s, BLOCK), :] for s in starts], axis=0)
        zero = jnp.zeros_like(win)
        return jnp.concatenate([jnp.where(low, win, zero), jnp.where(low, zero, win)], axis=0)

    def scores(qi):
        i = t * Q_BLOCKS + qi
        starts = [pl.multiple_of(jnp.clip((i + c) * BLOCK, 0, seq - BLOCK), BLOCK) for c in (-1, 0, 1)]
        q = q_ref[0, qi * BLOCK:(qi + 1) * BLOCK, :]
        qs = jnp.concatenate([q[:, p * KV_COLS:(p + 1) * KV_COLS] for p in range(N_PAIRS)], axis=0)
        logits = lax.dot_general(qs, block_diag(k_ref, starts), (((1,), (1,)), ((), ())),
                                 preferred_element_type=f32)
        variant = jnp.where(i == 0, 1, jnp.where(i == nblk - 1, 2, 0))
        return logits + bias_ref[variant], block_diag(v_ref, starts)

    def finish(qi, logits, vbd):
        probs, scales = [], []
        for p in range(N_PAIRS):
            halves, recips = [], []
            for side in range(2):
                l = logits[p * BLOCK:(p + 1) * BLOCK, side * keys:(side + 1) * keys]
                sink = sink_ref[2 * p + side]
                m = jnp.maximum(jnp.max(l, axis=-1, keepdims=True), sink)
                e = jnp.exp2(l - m)
                denom = jnp.sum(e, axis=-1, keepdims=True) + jnp.exp2(sink - m)
                halves.append(e.astype(bf16))
                recips.append(1.0 / denom)
            probs.append(jnp.concatenate(halves, axis=-1))
            scales.append(jnp.where(out_low, recips[0], recips[1]))
        pv = _dot(jnp.concatenate(probs, axis=0), vbd)
        out = [pv[p * BLOCK:(p + 1) * BLOCK, :] * scales[p] for p in range(N_PAIRS)]
        o_ref[0, qi * BLOCK:(qi + 1) * BLOCK, :] = jnp.concatenate(out, axis=-1).astype(bf16)

    staged = [scores(qi) for qi in range(Q_BLOCKS)]
    for qi in range(Q_BLOCKS):
        finish(qi, *staged[qi])


def _attention(qkv, bias, sink):
    B, S, _ = qkv.shape
    nblk = S // BLOCK
    assert nblk >= 2 and nblk % Q_BLOCKS == 0
    k_col = ATTN_W // KV_COLS
    rows = Q_BLOCKS * BLOCK
    return pl.pallas_call(
        functools.partial(_attn_kernel, nblk=nblk),
        grid=(B, nblk // Q_BLOCKS),
        in_specs=[
            pl.BlockSpec((1, rows, ATTN_W), lambda b, t: (b, t, 0)),
            pl.BlockSpec((1, S, KV_COLS), lambda b, t: (b, 0, k_col)),
            pl.BlockSpec((1, S, KV_COLS), lambda b, t: (b, 0, k_col + 1)),
            pl.BlockSpec((3, N_PAIRS * BLOCK, 6 * BLOCK), lambda b, t: (0, 0, 0)),
            pl.BlockSpec(memory_space=pltpu.SMEM),
        ],
        out_specs=pl.BlockSpec((1, rows, ATTN_W), lambda b, t: (b, t, 0)),
        out_shape=jax.ShapeDtypeStruct((B, S, ATTN_W), bf16),
        compiler_params=pltpu.CompilerParams(
            dimension_semantics=("arbitrary", "arbitrary"), vmem_limit_bytes=VMEM_LIMIT),
        name="attn",
    )(qkv, qkv, qkv, bias, sink)


def _bias_kernel(bucket_ref, rel_ref, table_ref, o_ref):
    bucket = bucket_ref[...]
    in_window = jnp.abs(rel_ref[...]) <= WINDOW
    for h in range(N_HEADS):
        acc = jnp.zeros(bucket.shape, f32)
        for b in range(NUM_BUCKETS):
            acc = jnp.where(bucket == b, table_ref[b, h], acc)
        o_ref[h] = jnp.where(in_window, acc * LOG2E, NEG)


def _band_bias(rel_table):
    rel = (jnp.arange(3 * BLOCK, dtype=jnp.int32)[None, :] - BLOCK) - jnp.arange(BLOCK, dtype=jnp.int32)[:, None]
    bias = pl.pallas_call(
        _bias_kernel,
        in_specs=[pl.BlockSpec(rel.shape, lambda: (0, 0)), pl.BlockSpec(rel.shape, lambda: (0, 0)),
                  pl.BlockSpec(memory_space=pltpu.SMEM)],
        out_specs=pl.BlockSpec((N_HEADS,) + rel.shape, lambda: (0, 0, 0)),
        out_shape=jax.ShapeDtypeStruct((N_HEADS,) + rel.shape, f32),
        name="bias_table",
    )(_t5_bucket(rel), rel, rel_table.astype(f32))
    chunk = jnp.arange(3 * BLOCK, dtype=jnp.int32) // BLOCK
    variants = jnp.stack([bias, jnp.where(chunk == 0, NEG, bias), jnp.where(chunk == 2, NEG, bias)])
    pairs = jnp.stack([variants[:, h] for h in HEAD_ORDER], axis=1)
    pairs = pairs.reshape(3, N_PAIRS, 2, BLOCK, 3 * BLOCK)
    return pairs.transpose(0, 1, 3, 2, 4).reshape(3, N_PAIRS * BLOCK, 6 * BLOCK)


def _fourier_rows(ys, ha_ref, hp_ref, cw_ref):
    a = _dot(ha_ref[...], ys).astype(bf16)
    p = _dot(hp_ref[...], ys).astype(bf16)
    four = []
    for g in range(F_GROUPS):
        cols = slice(g * F_CH, (g + 1) * F_CH)
        ap = jnp.concatenate([a[:, cols], p[:, cols]], axis=-1)
        four.append(_dot(ap, cw_ref[g]))
    return jnp.concatenate(four, axis=-1)


def _fold_kernel(cs_ref, wf_ref, o_ref):
    for g in range(F_GROUPS):
        o_ref[g] = jnp.dot(cs_ref[...], wf_ref[g], preferred_element_type=f32,
                           precision=lax.Precision.HIGHEST).astype(bf16)


def _fold_channel_dft(w_f):
    ar = jnp.arange(F_CH, dtype=jnp.int32)
    cc, sc = _trig(ar[:, None] * ar[None, :], F_CH)
    cs = jnp.concatenate([cc, -sc], axis=0) * F_CH ** -0.5
    return pl.pallas_call(
        _fold_kernel,
        out_shape=jax.ShapeDtypeStruct((F_GROUPS, 2 * F_CH, F_CH), bf16),
        name="fold_channel_dft",
    )(cs, w_f.astype(f32))


def _mix_kernel(y_ref, ha_ref, hp_ref, cw_ref, res_ref, attn_ref, ga_ref, gf_ref, wo_ref,
                l1g_ref, l1b_ref, x1_ref):
    rows = RADIX * SUB
    attn = attn_ref[0].astype(f32)
    for hh in range(JB // SUB):
        kk = slice(hh * SUB, (hh + 1) * SUB)
        ys = y_ref[0, :, :, kk, :].reshape(2 * rows, FOUR_W).astype(bf16)
        fn = _rms_norm(_fourier_rows(ys, ha_ref, hp_ref, cw_ref), gf_ref[...]).astype(bf16)
        an = _rms_norm(attn[:, kk, :].reshape(rows, ATTN_W), ga_ref[...]).astype(bf16)
        mix = _dot(an, wo_ref[:ATTN_W, :]) + _dot(fn, wo_ref[ATTN_W:, :])
        x1 = _layer_norm(res_ref[0, :, kk, :].reshape(rows, D_MODEL) + mix, l1g_ref[...], l1b_ref[...])
        x1_ref[0, :, kk, :] = x1.reshape(RADIX, SUB, D_MODEL)


def _mix(y, ha, hp, cw, res, attn, g_attn, g_four, w_o, ln1_g, ln1_b):
    B = y.shape[0]
    full = lambda *shape: pl.BlockSpec(shape, lambda b, kb: (0,) * len(shape))
    tile = lambda w: pl.BlockSpec((1, RADIX, JB, w), lambda b, kb: (b, 0, kb, 0))
    return pl.pallas_call(
        _mix_kernel,
        grid=(B, RADIX // JB),
        in_specs=[
            pl.BlockSpec((1, 2, RADIX, JB, FOUR_W), lambda b, kb: (b, 0, 0, kb, 0)),
            full(RADIX * SUB, 2 * RADIX * SUB), full(RADIX * SUB, 2 * RADIX * SUB),
            full(F_GROUPS, 2 * F_CH, F_CH),
            tile(D_MODEL), tile(ATTN_W),
            full(1, ATTN_W), full(1, FOUR_W), full(D_MODEL, D_MODEL),
            full(1, D_MODEL), full(1, D_MODEL),
        ],
        out_specs=tile(D_MODEL),
        out_shape=jax.ShapeDtypeStruct((B, RADIX, RADIX, D_MODEL), f32),
        compiler_params=pltpu.CompilerParams(
            dimension_semantics=("arbitrary", "arbitrary"), vmem_limit_bytes=VMEM_LIMIT),
        name="mix",
    )(y, ha, hp, cw, res, attn, g_attn, g_four, w_o, ln1_g, ln1_b)


def _gelu(x):
    return 0.5 * x * (1.0 + lax.erf(x * (1.0 / math.sqrt(2.0))))


def _ffn_kernel(xp_ref, xm_ref, xn_ref, wup_ref, cw_ref, cb_ref, wd_ref, l2g_ref, l2b_ref,
                y_ref, xs_ref, h_ref, act_ref, *, tiles_per_seq):
    t = pl.program_id(0)
    first = (t % tiles_per_seq) == 0
    last = (t % tiles_per_seq) == tiles_per_seq - 1
    xs_ref[0:HALO, :] = jnp.where(first, 0.0, xp_ref[...]).astype(bf16)
    xs_ref[HALO:HALO + ROW_TILE, :] = xm_ref[...].astype(bf16)
    xs_ref[HALO + ROW_TILE:, :] = jnp.where(last, 0.0, xn_ref[...]).astype(bf16)

    def conv(slot, col):
        cw = cw_ref[:, col:col + FF_CHUNK]
        prev = h_ref[slot, HALO - 1:HALO - 1 + ROW_TILE, :]
        cur = h_ref[slot, HALO:HALO + ROW_TILE, :]
        nxt = h_ref[slot, HALO + 1:HALO + 1 + ROW_TILE, :]
        return (cw[0:1, :] * prev + cw[1:2, :] * cur + cw[2:3, :] * nxt
                + cb_ref[:, col:col + FF_CHUNK])

    for c in range(N_FF_CHUNKS):
        sa, su = 2 * (c % 2), 2 * (c % 2) + 1
        col_a, col_u = c * FF_CHUNK, D_FF + c * FF_CHUNK
        h_ref[sa] = _dot(xs_ref[...], wup_ref[:, col_a:col_a + FF_CHUNK])
        h_ref[su] = _dot(xs_ref[...], wup_ref[:, col_u:col_u + FF_CHUNK])
        act = _gelu(conv(sa, col_a)) * conv(su, col_u)
        act_ref[:, col_a:col_a + FF_CHUNK] = act.astype(bf16)
    for rows in (slice(0, ROW_TILE // 2), slice(ROW_TILE // 2, ROW_TILE)):
        ffn = _dot(act_ref[rows, :], wd_ref[...])
        y_ref[rows, :] = _layer_norm(ALPHA * xm_ref[rows, :] + ffn, l2g_ref[...], l2b_ref[...])


def _ffn(x1, seq_len, w_up, conv_w, conv_b, w_down, ln2_g, ln2_b):
    T = x1.shape[0]
    tiles_per_seq = seq_len // ROW_TILE
    hb = ROW_TILE // HALO
    n_hb = T // HALO
    full = lambda *shape: pl.BlockSpec(shape, lambda t: (0,) * len(shape),
                                       pipeline_mode=pl.Buffered(1))
    m = ROW_TILE + 2 * HALO
    return pl.pallas_call(
        functools.partial(_ffn_kernel, tiles_per_seq=tiles_per_seq),
        grid=(T // ROW_TILE,),
        in_specs=[
            pl.BlockSpec((HALO, D_MODEL), lambda t: (jnp.maximum(t * hb - 1, 0), 0)),
            pl.BlockSpec((ROW_TILE, D_MODEL), lambda t: (t, 0)),
            pl.BlockSpec((HALO, D_MODEL), lambda t: (jnp.minimum((t + 1) * hb, n_hb - 1), 0)),
            full(D_MODEL, 2 * D_FF), full(3, 2 * D_FF), full(1, 2 * D_FF),
            full(D_FF, D_MODEL), full(1, D_MODEL), full(1, D_MODEL),
        ],
        out_specs=pl.BlockSpec((ROW_TILE, D_MODEL), lambda t: (t, 0)),
        out_shape=jax.ShapeDtypeStruct((T, D_MODEL), f32),
        scratch_shapes=[
            pltpu.VMEM((m, D_MODEL), bf16),
            pltpu.VMEM((4, m, FF_CHUNK), f32),
            pltpu.VMEM((ROW_TILE, D_FF), bf16),
        ],
        compiler_params=pltpu.CompilerParams(
            dimension_semantics=("arbitrary",), vmem_limit_bytes=VMEM_LIMIT),
        name="ffn",
    )(x1, x1, x1, w_up, conv_w, conv_b, w_down, ln2_g, ln2_b)


def _t5_bucket(rel):
    nb = NUM_BUCKETS // 2
    ret = jnp.where(rel > 0, nb, 0)
    n = jnp.abs(rel)
    max_exact = nb // 2
    nf = jnp.maximum(n, 1).astype(f32)
    large = max_exact + (jnp.log(nf / max_exact) / math.log(MAX_DISTANCE / max_exact)
                         * (nb - max_exact)).astype(jnp.int32)
    large = jnp.minimum(large, nb - 1)
    return ret + jnp.where(n < max_exact, n, large)


def _trig(num, den):
    ang = (num % den).astype(f32) * (2.0 * math.pi / den)
    return jnp.cos(ang), jnp.sin(ang)


def _stage1_tables(seq):
    shape = (RADIX // JB, JB // SUB, 2 * SUB * RADIX, RADIX * SUB)
    jb, hh, r, c = (lax.broadcasted_iota(jnp.int32, shape, d) for d in range(4))
    part, jj, k1 = r // (SUB * RADIX), (r // RADIX) % SUB, r % RADIX
    i, jj_in = c // SUB, c % SUB
    cos, _ = _trig(k1 * (RADIX * i + jb * JB + hh * SUB + jj) - part * (seq // 4), seq)
    return jnp.where(jj == jj_in, cos * (seq ** -0.5), 0.0).astype(bf16)


def _stage2_tables():
    shape = (RADIX * SUB, 2 * RADIX * SUB)
    r, c = (lax.broadcasted_iota(jnp.int32, shape, d) for d in range(2))
    k2, k1 = r // SUB, r % SUB
    part, j, k1_in = c // (RADIX * SUB), (c // SUB) % RADIX, c % SUB
    cos, sin = _trig(j * k2, RADIX)
    same = k1 == k1_in
    ha = jnp.where(same, jnp.where(part == 0, cos, -sin), 0.0)
    hp = jnp.where(same, jnp.where(part == 0, sin, cos), 0.0)
    return ha.astype(bf16), hp.astype(bf16)


def _encode(x, p):
    B, S, _ = x.shape
    assert S == RADIX * RADIX
    T = B * S
    qkv, y1, res = _in_proj(x, p["ln_in_g"], p["ln_in_b"], p["w_in"], p["dft1"])
    attn = _attention(qkv.reshape(B, S, QKV_COLS), p["bias"], p["sink"])
    x1 = _mix(y1, p["ha"], p["hp"], p["cw"], res, attn.reshape(B, RADIX, RADIX, ATTN_W),
              p["g_attn"], p["g_four"], p["w_o"], p["ln1_g"], p["ln1_b"])
    y = _ffn(x1.reshape(T, D_MODEL), S, p["w_up"], p["conv_w"], p["conv_b"], p["w_down"], p["ln2_g"], p["ln2_b"])
    return y.reshape(B, S, D_MODEL)


def kernel(x_prompt, x_sample, ln_in_g, ln_in_b, rel_table, w_in, attn_sink, w_fourier, g_attn,
           g_fourier, w_o, ln1_g, ln1_b, w_up, conv_w, conv_b, w_down, ln2_g, ln2_b):
    S = x_prompt.shape[1]
    row = lambda v: v.reshape(1, -1).astype(f32)
    col_scale = jnp.concatenate([jnp.full((ATTN_W,), HEAD_DIM ** -0.5 * LOG2E, f32),
                                 jnp.ones((IN_COLS - ATTN_W,), f32)])
    ha, hp = _stage2_tables()

    def by_pairs(w, axis):
        heads = jnp.split(w, N_HEADS, axis=axis)
        return jnp.concatenate([heads[h] for h in HEAD_ORDER], axis=axis)

    w_in_s = w_in[0] * col_scale
    w_in_p = jnp.concatenate([by_pairs(w_in_s[:, :ATTN_W], 1), w_in_s[:, ATTN_W:]], axis=1)
    w_o_p = jnp.concatenate([by_pairs(w_o[0][:ATTN_W], 0), w_o[0][ATTN_W:]], axis=0)

    p = dict(
        ln_in_g=row(ln_in_g), ln_in_b=row(ln_in_b),
        w_in=w_in_p.astype(bf16),
        bias=_band_bias(rel_table), sink=by_pairs(attn_sink[0].astype(f32), 0) * LOG2E,
        dft1=_stage1_tables(S), ha=ha, hp=hp,
        cw=_fold_channel_dft(w_fourier[0]),
        g_attn=row(by_pairs(g_attn[0], 0)), g_four=row(g_fourier[0]),
        w_o=w_o_p.astype(bf16),
        ln1_g=row(ln1_g[0]), ln1_b=row(ln1_b[0]),
        w_up=w_up[0].astype(bf16), conv_w=conv_w[0].astype(f32), conv_b=row(conv_b[0]),
        w_down=w_down[0].astype(bf16),
        ln2_g=row(ln2_g[0]), ln2_b=row(ln2_b[0]),
    )
    return (_encode(x_prompt, p), _encode(x_sample, p))
```

```python
import functools
import math

import jax
import jax.numpy as jnp
from jax import lax
from jax.experimental import pallas as pl
from jax.experimental.pallas import tpu as pltpu

D_MODEL = 1024
HEAD_DIM = 64
N_HEADS = 8
N_KV_HEADS = 2
GROUP = N_HEADS // N_KV_HEADS
WINDOW = 128
BLOCK = 128
NUM_BUCKETS = 32
MAX_DISTANCE = 128
F_GROUPS = 4
F_CH = 128
ATTN_W = N_HEADS * HEAD_DIM
FOUR_W = F_GROUPS * F_CH
KV_COLS = N_KV_HEADS * HEAD_DIM
QKV_COLS = ATTN_W + 2 * KV_COLS
IN_COLS = QKV_COLS + FOUR_W
D_FF = 2816
EPS = 1e-5
DEPTH = 1
ALPHA = (2.0 * DEPTH) ** 0.25
NEG = -1e30
LOG2E = math.log2(math.e)
N_PAIRS = N_HEADS // 2
HEAD_ORDER = tuple(h for p in range(N_PAIRS) for h in (p, GROUP + p))
Q_BLOCKS = 2

ROW_TILE = 512
OUT_PARTS = 2
FF_CHUNK = 256
N_FF_CHUNKS = D_FF // FF_CHUNK
HALO = 16
RADIX = 64
JB = 16
SUB = 8
VMEM_LIMIT = 56 * 1024 * 1024

bf16 = jnp.bfloat16
f32 = jnp.float32


def _layer_norm(x, g, b):
    mu = jnp.mean(x, axis=-1, keepdims=True)
    xc = x - mu
    var = jnp.mean(xc * xc, axis=-1, keepdims=True)
    return xc * lax.rsqrt(var + EPS) * g + b


def _rms_norm(x, g):
    return x * lax.rsqrt(jnp.mean(x * x, axis=-1, keepdims=True) + EPS) * g


def _dot(a, b):
    return jnp.dot(a, b, preferred_element_type=f32)


def _in_proj_kernel(x_ref, g_ref, b_ref, w_ref, dft_ref, qkv_ref, y_ref, res_ref):
    qkv = []
    for hh in range(JB // SUB):
        jj = slice(hh * SUB, (hh + 1) * SUB)
        x0 = _layer_norm(x_ref[0, :, jj, :].reshape(RADIX * SUB, D_MODEL), g_ref[...], b_ref[...])
        res_ref[0, :, jj, :] = (ALPHA * x0).reshape(RADIX, SUB, D_MODEL)
        h = _dot(x0.astype(bf16), w_ref[...])
        qkv.append(h[:, :QKV_COLS].reshape(RADIX, SUB, QKV_COLS))
        y = _dot(dft_ref[0, hh], h[:, QKV_COLS:].astype(bf16))
        y_ref[0, :, :, 0, hh * SUB * SUB:(hh + 1) * SUB * SUB, :] = (
            y.astype(bf16).reshape(RADIX // SUB, 2, SUB * SUB, FOUR_W))
    qkv_ref[0] = jnp.concatenate(qkv, axis=1).astype(bf16)


def _in_proj(x4, ln_g, ln_b, w_in, dft1):
    B = x4.shape[0]
    return pl.pallas_call(
        _in_proj_kernel,
        grid=(RADIX // JB, B),
        in_specs=[
            pl.BlockSpec((1, RADIX, JB, D_MODEL), lambda jb, b: (b, 0, jb, 0)),
            pl.BlockSpec((1, D_MODEL), lambda jb, b: (0, 0)),
            pl.BlockSpec((1, D_MODEL), lambda jb, b: (0, 0)),
            pl.BlockSpec((D_MODEL, IN_COLS), lambda jb, b: (0, 0)),
            pl.BlockSpec((1, JB // SUB, 2 * RADIX * SUB, RADIX * SUB), lambda jb, b: (jb, 0, 0, 0)),
        ],
        out_specs=[
            pl.BlockSpec((1, RADIX, JB, QKV_COLS), lambda jb, b: (b, 0, jb, 0)),
            pl.BlockSpec((1, RADIX // SUB, 2, 1, JB * SUB, FOUR_W), lambda jb, b: (b, 0, 0, jb, 0, 0)),
            pl.BlockSpec((1, RADIX, JB, D_MODEL), lambda jb, b: (b, 0, jb, 0)),
        ],
        out_shape=[
            jax.ShapeDtypeStruct((B, RADIX, RADIX, QKV_COLS), bf16),
            jax.ShapeDtypeStruct((B, RADIX // SUB, 2, RADIX // JB, JB * SUB, FOUR_W), bf16),
            jax.ShapeDtypeStruct((B, RADIX, RADIX, D_MODEL), f32),
        ],
        compiler_params=pltpu.CompilerParams(
            dimension_semantics=("arbitrary", "arbitrary"), vmem_limit_bytes=VMEM_LIMIT),
        name="in_proj",
    )(x4, ln_g, ln_b, w_in, dft1)


def _attn_kernel(q_ref, k_ref, v_ref, bias_ref, sink_ref, o_ref, *, nblk):
    t = pl.program_id(1)
    seq = nblk * BLOCK
    keys = 3 * BLOCK
    lane = lax.broadcasted_iota(jnp.int32, (keys, KV_COLS), 1)
    low = lane < HEAD_DIM
    out_low = lax.broadcasted_iota(jnp.int32, (BLOCK, KV_COLS), 1) < HEAD_DIM

    def block_diag(ref, starts):
        win = jnp.concatenate([ref[0, pl.ds(s, BLOCK), :] for s in starts], axis=0)
        zero = jnp.zeros_like(win)
        return jnp.concatenate([jnp.where(low, win, zero), jnp.where(low, zero, win)], axis=0)

    def scores(qi):
        i = t * Q_BLOCKS + qi
        starts = [pl.multiple_of(jnp.clip((i + c) * BLOCK, 0, seq - BLOCK), BLOCK) for c in (-1, 0, 1)]
        q = q_ref[0, qi * BLOCK:(qi + 1) * BLOCK, :]
        qs = jnp.concatenate([q[:, p * KV_COLS:(p + 1) * KV_COLS] for p in range(N_PAIRS)], axis=0)
        logits = lax.dot_general(qs, block_diag(k_ref, starts), (((1,), (1,)), ((), ())),
                                 preferred_element_type=f32)
        variant = jnp.where(i == 0, 1, jnp.where(i == nblk - 1, 2, 0))
        return logits + bias_ref[variant], block_diag(v_ref, starts)

    def finish(qi, logits, vbd):
        probs, scales = [], []
        for p in range(N_PAIRS):
            halves, recips = [], []
            for side in range(2):
                l = logits[p * BLOCK:(p + 1) * BLOCK, side * keys:(side + 1) * keys]
                sink = sink_ref[2 * p + side]
                m = jnp.maximum(jnp.max(l, axis=-1, keepdims=True), sink)
                e = jnp.exp2(l - m)
                denom = jnp.sum(e, axis=-1, keepdims=True) + jnp.exp2(sink - m)
                halves.append(e.astype(bf16))
                recips.append(1.0 / denom)
            probs.append(jnp.concatenate(halves, axis=-1))
            scales.append(jnp.where(out_low, recips[0], recips[1]))
        pv = _dot(jnp.concatenate(probs, axis=0), vbd)
        out = [pv[p * BLOCK:(p + 1) * BLOCK, :] * scales[p] for p in range(N_PAIRS)]
        o_ref[0, qi * BLOCK:(qi + 1) * BLOCK, :] = jnp.concatenate(out, axis=-1).astype(bf16)

    staged = [scores(qi) for qi in range(Q_BLOCKS)]
    for qi in range(Q_BLOCKS):
        finish(qi, *staged[qi])


def _attention(qkv, bias, sink):
    B, S, _ = qkv.shape
    nblk = S // BLOCK
    assert nblk >= 2 and nblk % Q_BLOCKS == 0
    k_col = ATTN_W // KV_COLS
    rows = Q_BLOCKS * BLOCK
    return pl.pallas_call(
        functools.partial(_attn_kernel, nblk=nblk),
        grid=(B, nblk // Q_BLOCKS),
        in_specs=[
            pl.BlockSpec((1, rows, ATTN_W), lambda b, t: (b, t, 0)),
            pl.BlockSpec((1, S, KV_COLS), lambda b, t: (b, 0, k_col)),
            pl.BlockSpec((1, S, KV_COLS), lambda b, t: (b, 0, k_col + 1)),
            pl.BlockSpec((3, N_PAIRS * BLOCK, 6 * BLOCK), lambda b, t: (0, 0, 0)),
            pl.BlockSpec(memory_space=pltpu.SMEM),
        ],
        out_specs=pl.BlockSpec((1, rows, ATTN_W), lambda b, t: (b, t, 0)),
        out_shape=jax.ShapeDtypeStruct((B, S, ATTN_W), bf16),
        compiler_params=pltpu.CompilerParams(
            dimension_semantics=("arbitrary", "arbitrary"), vmem_limit_bytes=VMEM_LIMIT),
        name="attn",
    )(qkv, qkv, qkv, bias, sink)


def _bias_kernel(bucket_ref, rel_ref, table_ref, o_ref):
    bucket = bucket_ref[...]
    in_window = jnp.abs(rel_ref[...]) <= WINDOW
    for h in range(N_HEADS):
        acc = jnp.zeros(bucket.shape, f32)
        for b in range(NUM_BUCKETS):
            acc = jnp.where(bucket == b, table_ref[b, h], acc)
        o_ref[h] = jnp.where(in_window, acc * LOG2E, NEG)


def _band_bias(rel_table):
    rel = (jnp.arange(3 * BLOCK, dtype=jnp.int32)[None, :] - BLOCK) - jnp.arange(BLOCK, dtype=jnp.int32)[:, None]
    bias = pl.pallas_call(
        _bias_kernel,
        in_specs=[pl.BlockSpec(rel.shape, lambda: (0, 0)), pl.BlockSpec(rel.shape, lambda: (0, 0)),
                  pl.BlockSpec(memory_space=pltpu.SMEM)],
        out_specs=pl.BlockSpec((N_HEADS,) + rel.shape, lambda: (0, 0, 0)),
        out_shape=jax.ShapeDtypeStruct((N_HEADS,) + rel.shape, f32),
        name="bias_table",
    )(_t5_bucket(rel), rel, rel_table.astype(f32))
    chunk = jnp.arange(3 * BLOCK, dtype=jnp.int32) // BLOCK
    variants = jnp.stack([bias, jnp.where(chunk == 0, NEG, bias), jnp.where(chunk == 2, NEG, bias)])
    pairs = jnp.stack([variants[:, h] for h in HEAD_ORDER], axis=1)
    pairs = pairs.reshape(3, N_PAIRS, 2, BLOCK, 3 * BLOCK)
    return pairs.transpose(0, 1, 3, 2, 4).reshape(3, N_PAIRS * BLOCK, 6 * BLOCK)


def _fourier_rows(ys, ha_ref, hp_ref, cw_ref):
    a = _dot(ha_ref[...], ys).astype(bf16)
    p = _dot(hp_ref[...], ys).astype(bf16)
    four = []
    for g in range(F_GROUPS):
        cols = slice(g * F_CH, (g + 1) * F_CH)
        ap = jnp.concatenate([a[:, cols], p[:, cols]], axis=-1)
        four.append(_dot(ap, cw_ref[g]))
    return jnp.concatenate(four, axis=-1)


def _fold_kernel(cs_ref, wf_ref, o_ref):
    for g in range(F_GROUPS):
        o_ref[g] = jnp.dot(cs_ref[...], wf_ref[g], preferred_element_type=f32,
                           precision=lax.Precision.HIGHEST).astype(bf16)


def _fold_channel_dft(w_f):
    ar = jnp.arange(F_CH, dtype=jnp.int32)
    cc, sc = _trig(ar[:, None] * ar[None, :], F_CH)
    cs = jnp.concatenate([cc, -sc], axis=0) * F_CH ** -0.5
    return pl.pallas_call(
        _fold_kernel,
        out_shape=jax.ShapeDtypeStruct((F_GROUPS, 2 * F_CH, F_CH), bf16),
        name="fold_channel_dft",
    )(cs, w_f.astype(f32))


def _mix_kernel(y_ref, ha_ref, hp_ref, cw_ref, res_ref, attn_ref, ga_ref, gf_ref,
                wo_ref, l1g_ref, l1b_ref, x1_ref):
    rows = RADIX * SUB
    attn = attn_ref[0].astype(f32)
    for hh in range(JB // SUB):
        kk = slice(hh * SUB, (hh + 1) * SUB)
        ys = y_ref[0, hh].reshape(2 * rows, FOUR_W)
        fn = _rms_norm(_fourier_rows(ys, ha_ref, hp_ref, cw_ref), gf_ref[...]).astype(bf16)
        an = _rms_norm(attn[:, kk, :].reshape(rows, ATTN_W), ga_ref[...]).astype(bf16)
        mix = _dot(an, wo_ref[:ATTN_W, :]) + _dot(fn, wo_ref[ATTN_W:, :])
        x1 = _layer_norm(res_ref[0, :, kk, :].reshape(rows, D_MODEL) + mix, l1g_ref[...], l1b_ref[...])
        x1_ref[0, :, kk, :] = x1.reshape(RADIX, SUB, D_MODEL)


def _mix(y, ha, hp, cw, res, attn, g_attn, g_four, w_o, ln1_g, ln1_b):
    B = y.shape[0]
    groups = JB // SUB
    full = lambda *shape: pl.BlockSpec(shape, lambda b, kb: (0,) * len(shape))
    tile = lambda w: pl.BlockSpec((1, RADIX, JB, w), lambda b, kb: (b, 0, kb, 0))
    return pl.pallas_call(
        _mix_kernel,
        grid=(B, RADIX // JB),
        in_specs=[
            pl.BlockSpec((1, groups, 2, RADIX * SUB, FOUR_W), lambda b, kb: (b, kb, 0, 0, 0)),
            full(RADIX * SUB, 2 * RADIX * SUB), full(RADIX * SUB, 2 * RADIX * SUB),
            full(F_GROUPS, 2 * F_CH, F_CH),
            tile(D_MODEL), tile(ATTN_W),
            full(1, ATTN_W), full(1, FOUR_W), full(D_MODEL, D_MODEL),
            full(1, D_MODEL), full(1, D_MODEL),
        ],
        out_specs=tile(D_MODEL),
        out_shape=jax.ShapeDtypeStruct((B, RADIX, RADIX, D_MODEL), f32),
        compiler_params=pltpu.CompilerParams(
            dimension_semantics=("arbitrary", "arbitrary"), vmem_limit_bytes=VMEM_LIMIT),
        name="mix",
    )(y, ha, hp, cw, res, attn, g_attn, g_four, w_o, ln1_g, ln1_b)


def _gelu(x):
    return 0.5 * x * (1.0 + lax.erf(x * (1.0 / math.sqrt(2.0))))


def _ffn_kernel(xp_ref, xm_ref, xn_ref, wup_ref, cw_ref, cb_ref, wd_ref, l2g_ref, l2b_ref,
                y_ref, xs_ref, h_ref, act_ref, *, tiles_per_seq):
    t = pl.program_id(0)
    first = (t % tiles_per_seq) == 0
    last = (t % tiles_per_seq) == tiles_per_seq - 1
    xs_ref[0:HALO, :] = jnp.where(first, 0.0, xp_ref[...]).astype(bf16)
    xs_ref[HALO:HALO + ROW_TILE, :] = xm_ref[...].astype(bf16)
    xs_ref[HALO + ROW_TILE:, :] = jnp.where(last, 0.0, xn_ref[...]).astype(bf16)

    def conv(slot, col):
        cw = cw_ref[:, col:col + FF_CHUNK]
        prev = h_ref[slot, HALO - 1:HALO - 1 + ROW_TILE, :]
        cur = h_ref[slot, HALO:HALO + ROW_TILE, :]
        nxt = h_ref[slot, HALO + 1:HALO + 1 + ROW_TILE, :]
        return (cw[0:1, :] * prev + cw[1:2, :] * cur + cw[2:3, :] * nxt
                + cb_ref[:, col:col + FF_CHUNK])

    for c in range(N_FF_CHUNKS):
        sa, su = 2 * (c % 2), 2 * (c % 2) + 1
        col_a, col_u = c * FF_CHUNK, D_FF + c * FF_CHUNK
        h_ref[sa] = _dot(xs_ref[...], wup_ref[:, col_a:col_a + FF_CHUNK])
        h_ref[su] = _dot(xs_ref[...], wup_ref[:, col_u:col_u + FF_CHUNK])
        act = _gelu(conv(sa, col_a)) * conv(su, col_u)
        act_ref[:, col_a:col_a + FF_CHUNK] = act.astype(bf16)
    part = ROW_TILE // OUT_PARTS
    for rows in (slice(r * part, (r + 1) * part) for r in range(OUT_PARTS)):
        ffn = _dot(act_ref[rows, :], wd_ref[...])
        y_ref[rows, :] = _layer_norm(ALPHA * xm_ref[rows, :] + ffn, l2g_ref[...], l2b_ref[...])


def _ffn(x1, seq_len, w_up, conv_w, conv_b, w_down, ln2_g, ln2_b):
    T = x1.shape[0]
    tiles_per_seq = seq_len // ROW_TILE
    hb = ROW_TILE // HALO
    n_hb = T // HALO
    full = lambda *shape: pl.BlockSpec(shape, lambda t: (0,) * len(shape),
                                       pipeline_mode=pl.Buffered(1))
    m = ROW_TILE + 2 * HALO
    return pl.pallas_call(
        functools.partial(_ffn_kernel, tiles_per_seq=tiles_per_seq),
        grid=(T // ROW_TILE,),
        in_specs=[
            pl.BlockSpec((HALO, D_MODEL), lambda t: (jnp.maximum(t * hb - 1, 0), 0)),
            pl.BlockSpec((ROW_TILE, D_MODEL), lambda t: (t, 0)),
            pl.BlockSpec((HALO, D_MODEL), lambda t: (jnp.minimum((t + 1) * hb, n_hb - 1), 0)),
            full(D_MODEL, 2 * D_FF), full(3, 2 * D_FF), full(1, 2 * D_FF),
            full(D_FF, D_MODEL), full(1, D_MODEL), full(1, D_MODEL),
        ],
        out_specs=pl.BlockSpec((ROW_TILE, D_MODEL), lambda t: (t, 0)),
        out_shape=jax.ShapeDtypeStruct((T, D_MODEL), f32),
        scratch_shapes=[
            pltpu.VMEM((m, D_MODEL), bf16),
            pltpu.VMEM((4, m, FF_CHUNK), f32),
            pltpu.VMEM((ROW_TILE, D_FF), bf16),
        ],
        compiler_params=pltpu.CompilerParams(
            dimension_semantics=("arbitrary",), vmem_limit_bytes=VMEM_LIMIT),
        name="ffn",
    )(x1, x1, x1, w_up, conv_w, conv_b, w_down, ln2_g, ln2_b)


def _t5_bucket(rel):
    nb = NUM_BUCKETS // 2
    ret = jnp.where(rel > 0, nb, 0)
    n = jnp.abs(rel)
    max_exact = nb // 2
    nf = jnp.maximum(n, 1).astype(f32)
    large = max_exact + (jnp.log(nf / max_exact) / math.log(MAX_DISTANCE / max_exact)
                         * (nb - max_exact)).astype(jnp.int32)
    large = jnp.minimum(large, nb - 1)
    return ret + jnp.where(n < max_exact, n, large)


def _trig(num, den):
    ang = (num % den).astype(f32) * (2.0 * math.pi / den)
    return jnp.cos(ang), jnp.sin(ang)


def _stage1_tables(seq):
    shape = (RADIX // JB, JB // SUB, RADIX // SUB, 2, SUB, SUB, RADIX)
    jb, hh, kb, part, jj, k1, i = (lax.broadcasted_iota(jnp.int32, shape, d) for d in range(7))
    phase = (kb * SUB + k1) * (RADIX * i + jb * JB + hh * SUB + jj) - part * (seq // 4)
    dense = (_trig(phase, seq)[0] * (seq ** -0.5)).astype(bf16).reshape(-1, RADIX)
    rows, cols = dense.shape[0], RADIX * SUB
    spread = (lax.broadcasted_iota(jnp.int32, (RADIX, cols), 1) // SUB
              == lax.broadcasted_iota(jnp.int32, (RADIX, cols), 0)).astype(bf16)
    wide = jnp.dot(dense, spread, preferred_element_type=bf16)
    keep = ((lax.broadcasted_iota(jnp.int32, (rows, cols), 0) // SUB) % SUB
            == lax.broadcasted_iota(jnp.int32, (rows, cols), 1) % SUB)
    return jnp.where(keep, wide, 0).reshape(RADIX // JB, JB // SUB, 2 * SUB * RADIX, cols)


def _stage2_tables():
    shape = (RADIX * SUB, 2 * RADIX * SUB)
    r, c = (lax.broadcasted_iota(jnp.int32, shape, d) for d in range(2))
    k2, k1 = r // SUB, r % SUB
    part, j, k1_in = c // (RADIX * SUB), (c // SUB) % RADIX, c % SUB
    cos, sin = _trig(j * k2, RADIX)
    same = k1 == k1_in
    ha = jnp.where(same, jnp.where(part == 0, cos, -sin), 0.0)
    hp = jnp.where(same, jnp.where(part == 0, sin, cos), 0.0)
    return ha.astype(bf16), hp.astype(bf16)


def _encode(x, p):
    B, S, _ = x.shape
    assert S == RADIX * RADIX
    T = B * S
    x4 = x.reshape(B, RADIX, RADIX, D_MODEL)
    qkv, y1, res = _in_proj(x4, p["ln_in_g"], p["ln_in_b"], p["w_in"], p["dft1"])
    attn = _attention(qkv.reshape(B, S, QKV_COLS), p["bias"], p["sink"])
    x1 = _mix(y1.reshape(B, RADIX // SUB, 2, RADIX * SUB, FOUR_W), p["ha"], p["hp"], p["cw"],
              res, attn.reshape(B, RADIX, RADIX, ATTN_W),
              p["g_attn"], p["g_four"], p["w_o"], p["ln1_g"], p["ln1_b"])
    y = _ffn(x1.reshape(T, D_MODEL), S, p["w_up"], p["conv_w"], p["conv_b"], p["w_down"], p["ln2_g"], p["ln2_b"])
    return y.reshape(B, S, D_MODEL)


def kernel(x_prompt, x_sample, ln_in_g, ln_in_b, rel_table, w_in, attn_sink, w_fourier, g_attn,
           g_fourier, w_o, ln1_g, ln1_b, w_up, conv_w, conv_b, w_down, ln2_g, ln2_b):
    S = x_prompt.shape[1]
    row = lambda v: v.reshape(1, -1).astype(f32)
    col_scale = jnp.concatenate([jnp.full((ATTN_W,), HEAD_DIM ** -0.5 * LOG2E, f32),
                                 jnp.ones((IN_COLS - ATTN_W,), f32)])
    ha, hp = _stage2_tables()

    def by_pairs(w, axis):
        heads = jnp.split(w, N_HEADS, axis=axis)
        return jnp.concatenate([heads[h] for h in HEAD_ORDER], axis=axis)

    w_in_s = w_in[0] * col_scale
    w_in_p = jnp.concatenate([by_pairs(w_in_s[:, :ATTN_W], 1), w_in_s[:, ATTN_W:]], axis=1)
    w_o_p = jnp.concatenate([by_pairs(w_o[0][:ATTN_W], 0), w_o[0][ATTN_W:]], axis=0)

    p = dict(
        ln_in_g=row(ln_in_g), ln_in_b=row(ln_in_b),
        w_in=w_in_p.astype(bf16),
        bias=_band_bias(rel_table), sink=by_pairs(attn_sink[0].astype(f32), 0) * LOG2E,
        dft1=_stage1_tables(S), ha=ha, hp=hp,
        cw=_fold_channel_dft(w_fourier[0]),
        g_attn=row(by_pairs(g_attn[0], 0)), g_four=row(g_fourier[0]),
        w_o=w_o_p.astype(bf16),
        ln1_g=row(ln1_g[0]), ln1_b=row(ln1_b[0]),
        w_up=w_up[0].astype(bf16), conv_w=conv_w[0].astype(f32), conv_b=row(conv_b[0]),
        w_down=w_down[0].astype(bf16),
        ln2_g=row(ln2_g[0]), ln2_b=row(ln2_b[0]),
    )
    return (_encode(x_prompt, p), _encode(x_sample, p))
```

```python
import functools
import math

import jax
import jax.numpy as jnp
from jax import lax
from jax.experimental import pallas as pl
from jax.experimental.pallas import tpu as pltpu

D_MODEL = 1024
HEAD_DIM = 64
N_HEADS = 8
N_KV_HEADS = 2
GROUP = N_HEADS // N_KV_HEADS
WINDOW = 128
BLOCK = 128
NUM_BUCKETS = 32
MAX_DISTANCE = 128
F_GROUPS = 4
F_CH = 128
ATTN_W = N_HEADS * HEAD_DIM
FOUR_W = F_GROUPS * F_CH
KV_COLS = N_KV_HEADS * HEAD_DIM
QKV_COLS = ATTN_W + 2 * KV_COLS
IN_COLS = QKV_COLS + FOUR_W
D_FF = 2816
EPS = 1e-5
DEPTH = 1
ALPHA = (2.0 * DEPTH) ** 0.25
NEG = -1e30
LOG2E = math.log2(math.e)
N_PAIRS = N_HEADS // 2
HEAD_ORDER = tuple(h for p in range(N_PAIRS) for h in (p, GROUP + p))
Q_BLOCKS = 16

ROW_TILE = 512
OUT_PARTS = 2
FF_CHUNK = 256
N_FF_CHUNKS = D_FF // FF_CHUNK
HALO = 16
RADIX = 64
JB = 16
SUB = 8
VMEM_LIMIT = 56 * 1024 * 1024

bf16 = jnp.bfloat16
f32 = jnp.float32


def _layer_norm(x, g, b):
    mu = jnp.mean(x, axis=-1, keepdims=True)
    xc = x - mu
    var = jnp.mean(xc * xc, axis=-1, keepdims=True)
    return xc * lax.rsqrt(var + EPS) * g + b


def _rms_norm(x, g):
    return x * lax.rsqrt(jnp.mean(x * x, axis=-1, keepdims=True) + EPS) * g


def _dot(a, b):
    return jnp.dot(a, b, preferred_element_type=f32)


def _in_proj_kernel(x_ref, g_ref, b_ref, w_ref, dft_ref, qkv_ref, y_ref, res_ref):
    qkv = []
    for hh in range(JB // SUB):
        jj = slice(hh * SUB, (hh + 1) * SUB)
        x0 = _layer_norm(x_ref[0, :, jj, :].reshape(RADIX * SUB, D_MODEL), g_ref[...], b_ref[...])
        res_ref[0, :, jj, :] = (ALPHA * x0).reshape(RADIX, SUB, D_MODEL)
        h = _dot(x0.astype(bf16), w_ref[...])
        qkv.append(h[:, :QKV_COLS].reshape(RADIX, SUB, QKV_COLS))
        y = _dot(dft_ref[0, hh], h[:, QKV_COLS:].astype(bf16))
        y_ref[0, :, :, 0, hh * SUB * SUB:(hh + 1) * SUB * SUB, :] = (
            y.astype(bf16).reshape(RADIX // SUB, 2, SUB * SUB, FOUR_W))
    qkv_ref[0] = jnp.concatenate(qkv, axis=1).astype(bf16)


def _in_proj(x4, ln_g, ln_b, w_in, dft1):
    B = x4.shape[0]
    return pl.pallas_call(
        _in_proj_kernel,
        grid=(RADIX // JB, B),
        in_specs=[
            pl.BlockSpec((1, RADIX, JB, D_MODEL), lambda jb, b: (b, 0, jb, 0)),
            pl.BlockSpec((1, D_MODEL), lambda jb, b: (0, 0)),
            pl.BlockSpec((1, D_MODEL), lambda jb, b: (0, 0)),
            pl.BlockSpec((D_MODEL, IN_COLS), lambda jb, b: (0, 0)),
            pl.BlockSpec((1, JB // SUB, 2 * RADIX * SUB, RADIX * SUB), lambda jb, b: (jb, 0, 0, 0)),
        ],
        out_specs=[
            pl.BlockSpec((1, RADIX, JB, QKV_COLS), lambda jb, b: (b, 0, jb, 0)),
            pl.BlockSpec((1, RADIX // SUB, 2, 1, JB * SUB, FOUR_W), lambda jb, b: (b, 0, 0, jb, 0, 0)),
            pl.BlockSpec((1, RADIX, JB, D_MODEL), lambda jb, b: (b, 0, jb, 0)),
        ],
        out_shape=[
            jax.ShapeDtypeStruct((B, RADIX, RADIX, QKV_COLS), bf16),
            jax.ShapeDtypeStruct((B, RADIX // SUB, 2, RADIX // JB, JB * SUB, FOUR_W), bf16),
            jax.ShapeDtypeStruct((B, RADIX, RADIX, D_MODEL), f32),
        ],
        compiler_params=pltpu.CompilerParams(
            dimension_semantics=("arbitrary", "arbitrary"), vmem_limit_bytes=VMEM_LIMIT),
        name="in_proj",
    )(x4, ln_g, ln_b, w_in, dft1)


def _attn_kernel(q_ref, k_ref, v_ref, bias_ref, sink_ref, o_ref, *, nblk):
    t = pl.program_id(1)
    seq = nblk * BLOCK
    keys = 3 * BLOCK
    lane = lax.broadcasted_iota(jnp.int32, (keys, KV_COLS), 1)
    low = lane < HEAD_DIM
    out_low = lax.broadcasted_iota(jnp.int32, (BLOCK, KV_COLS), 1) < HEAD_DIM

    def block_diag(ref, starts):
        win = jnp.concatenate([ref[0, pl.ds(s, BLOCK), :] for s in starts], axis=0)
        zero = jnp.zeros_like(win)
        return jnp.concatenate([jnp.where(low, win, zero), jnp.where(low, zero, win)], axis=0)

    def scores(qi):
        i = t * Q_BLOCKS + qi
        starts = [pl.multiple_of(jnp.clip((i + c) * BLOCK, 0, seq - BLOCK), BLOCK) for c in (-1, 0, 1)]
        q = q_ref[0, qi * BLOCK:(qi + 1) * BLOCK, :]
        qs = jnp.concatenate([q[:, p * KV_COLS:(p + 1) * KV_COLS] for p in range(N_PAIRS)], axis=0)
        logits = lax.dot_general(qs, block_diag(k_ref, starts), (((1,), (1,)), ((), ())),
                                 preferred_element_type=f32)
        variant = jnp.where(i == 0, 1, jnp.where(i == nblk - 1, 2, 0))
        return logits + bias_ref[variant], block_diag(v_ref, starts)

    def finish(qi, logits, vbd):
        probs, scales = [], []
        for p in range(N_PAIRS):
            halves, recips = [], []
            for side in range(2):
                l = logits[p * BLOCK:(p + 1) * BLOCK, side * keys:(side + 1) * keys]
                sink = sink_ref[2 * p + side]
                m = jnp.maximum(jnp.max(l, axis=-1, keepdims=True), sink)
                e = jnp.exp2(l - m)
                denom = jnp.sum(e, axis=-1, keepdims=True) + jnp.exp2(sink - m)
                halves.append(e.astype(bf16))
                recips.append(1.0 / denom)
            probs.append(jnp.concatenate(halves, axis=-1))
            scales.append(jnp.where(out_low, recips[0], recips[1]))
        pv = _dot(jnp.concatenate(probs, axis=0), vbd)
        out = [pv[p * BLOCK:(p + 1) * BLOCK, :] * scales[p] for p in range(N_PAIRS)]
        o_ref[0, qi * BLOCK:(qi + 1) * BLOCK, :] = jnp.concatenate(out, axis=-1).astype(bf16)

    staged = scores(0)
    for qi in range(Q_BLOCKS):
        ahead = scores(qi + 1) if qi + 1 < Q_BLOCKS else None
        finish(qi, *staged)
        staged = ahead


def _attention(qkv, bias, sink):
    B, S, _ = qkv.shape
    nblk = S // BLOCK
    assert nblk >= 2 and nblk % Q_BLOCKS == 0
    k_col = ATTN_W // KV_COLS
    rows = Q_BLOCKS * BLOCK
    return pl.pallas_call(
        functools.partial(_attn_kernel, nblk=nblk),
        grid=(B, nblk // Q_BLOCKS),
        in_specs=[
            pl.BlockSpec((1, rows, ATTN_W), lambda b, t: (b, t, 0)),
            pl.BlockSpec((1, S, KV_COLS), lambda b, t: (b, 0, k_col)),
            pl.BlockSpec((1, S, KV_COLS), lambda b, t: (b, 0, k_col + 1)),
            pl.BlockSpec((3, N_PAIRS * BLOCK, 6 * BLOCK), lambda b, t: (0, 0, 0)),
            pl.BlockSpec(memory_space=pltpu.SMEM),
        ],
        out_specs=pl.BlockSpec((1, rows, ATTN_W), lambda b, t: (b, t, 0)),
        out_shape=jax.ShapeDtypeStruct((B, S, ATTN_W), bf16),
        compiler_params=pltpu.CompilerParams(
            dimension_semantics=("arbitrary", "arbitrary"), vmem_limit_bytes=VMEM_LIMIT),
        name="attn",
    )(qkv, qkv, qkv, bias, sink)


def _bias_kernel(bucket_ref, rel_ref, table_ref, o_ref):
    bucket = bucket_ref[...]
    in_window = jnp.abs(rel_ref[...]) <= WINDOW
    for h in range(N_HEADS):
        acc = jnp.zeros(bucket.shape, f32)
        for b in range(NUM_BUCKETS):
            acc = jnp.where(bucket == b, table_ref[b, h], acc)
        o_ref[h] = jnp.where(in_window, acc * LOG2E, NEG)


def _band_bias(rel_table):
    rel = (jnp.arange(3 * BLOCK, dtype=jnp.int32)[None, :] - BLOCK) - jnp.arange(BLOCK, dtype=jnp.int32)[:, None]
    bias = pl.pallas_call(
        _bias_kernel,
        in_specs=[pl.BlockSpec(rel.shape, lambda: (0, 0)), pl.BlockSpec(rel.shape, lambda: (0, 0)),
                  pl.BlockSpec(memory_space=pltpu.SMEM)],
        out_specs=pl.BlockSpec((N_HEADS,) + rel.shape, lambda: (0, 0, 0)),
        out_shape=jax.ShapeDtypeStruct((N_HEADS,) + rel.shape, f32),
        name="bias_table",
    )(_t5_bucket(rel), rel, rel_table.astype(f32))
    chunk = jnp.arange(3 * BLOCK, dtype=jnp.int32) // BLOCK
    variants = jnp.stack([bias, jnp.where(chunk == 0, NEG, bias), jnp.where(chunk == 2, NEG, bias)])
    pairs = jnp.stack([variants[:, h] for h in HEAD_ORDER], axis=1)
    pairs = pairs.reshape(3, N_PAIRS, 2, BLOCK, 3 * BLOCK)
    return pairs.transpose(0, 1, 3, 2, 4).reshape(3, N_PAIRS * BLOCK, 6 * BLOCK)


def _fourier_rows(ys, ha_ref, hp_ref, cw_ref):
    a = _dot(ha_ref[...], ys).astype(bf16)
    p = _dot(hp_ref[...], ys).astype(bf16)
    four = []
    for g in range(F_GROUPS):
        cols = slice(g * F_CH, (g + 1) * F_CH)
        ap = jnp.concatenate([a[:, cols], p[:, cols]], axis=-1)
        four.append(_dot(ap, cw_ref[g]))
    return jnp.concatenate(four, axis=-1)


def _fold_kernel(cs_ref, wf_ref, o_ref):
    for g in range(F_GROUPS):
        o_ref[g] = jnp.dot(cs_ref[...], wf_ref[g], preferred_element_type=f32,
                           precision=lax.Precision.HIGHEST).astype(bf16)


def _fold_channel_dft(w_f):
    ar = jnp.arange(F_CH, dtype=jnp.int32)
    cc, sc = _trig(ar[:, None] * ar[None, :], F_CH)
    cs = jnp.concatenate([cc, -sc], axis=0) * F_CH ** -0.5
    return pl.pallas_call(
        _fold_kernel,
        out_shape=jax.ShapeDtypeStruct((F_GROUPS, 2 * F_CH, F_CH), bf16),
        name="fold_channel_dft",
    )(cs, w_f.astype(f32))


def _mix_kernel(y_ref, ha_ref, hp_ref, cw_ref, res_ref, attn_ref, ga_ref, gf_ref,
                wo_ref, l1g_ref, l1b_ref, x1_ref):
    rows = RADIX * SUB
    attn = attn_ref[0].astype(f32)
    for hh in range(JB // SUB):
        kk = slice(hh * SUB, (hh + 1) * SUB)
        ys = y_ref[0, hh].reshape(2 * rows, FOUR_W)
        fn = _rms_norm(_fourier_rows(ys, ha_ref, hp_ref, cw_ref), gf_ref[...]).astype(bf16)
        an = _rms_norm(attn[:, kk, :].reshape(rows, ATTN_W), ga_ref[...]).astype(bf16)
        mix = _dot(an, wo_ref[:ATTN_W, :]) + _dot(fn, wo_ref[ATTN_W:, :])
        x1 = _layer_norm(res_ref[0, :, kk, :].reshape(rows, D_MODEL) + mix, l1g_ref[...], l1b_ref[...])
        x1_ref[0, :, kk, :] = x1.reshape(RADIX, SUB, D_MODEL)


def _mix(y, ha, hp, cw, res, attn, g_attn, g_four, w_o, ln1_g, ln1_b):
    B = y.shape[0]
    groups = JB // SUB
    full = lambda *shape: pl.BlockSpec(shape, lambda b, kb: (0,) * len(shape))
    tile = lambda w: pl.BlockSpec((1, RADIX, JB, w), lambda b, kb: (b, 0, kb, 0))
    return pl.pallas_call(
        _mix_kernel,
        grid=(B, RADIX // JB),
        in_specs=[
            pl.BlockSpec((1, groups, 2, RADIX * SUB, FOUR_W), lambda b, kb: (b, kb, 0, 0, 0)),
            full(RADIX * SUB, 2 * RADIX * SUB), full(RADIX * SUB, 2 * RADIX * SUB),
            full(F_GROUPS, 2 * F_CH, F_CH),
            tile(D_MODEL), tile(ATTN_W),
            full(1, ATTN_W), full(1, FOUR_W), full(D_MODEL, D_MODEL),
            full(1, D_MODEL), full(1, D_MODEL),
        ],
        out_specs=tile(D_MODEL),
        out_shape=jax.ShapeDtypeStruct((B, RADIX, RADIX, D_MODEL), f32),
        compiler_params=pltpu.CompilerParams(
            dimension_semantics=("arbitrary", "arbitrary"), vmem_limit_bytes=VMEM_LIMIT),
        name="mix",
    )(y, ha, hp, cw, res, attn, g_attn, g_four, w_o, ln1_g, ln1_b)


def _gelu(x):
    return 0.5 * x * (1.0 + lax.erf(x * (1.0 / math.sqrt(2.0))))


def _ffn_kernel(xp_ref, xm_ref, xn_ref, wup_ref, cw_ref, cb_ref, wd_ref, l2g_ref, l2b_ref,
                y_ref, xs_ref, h_ref, act_ref, *, tiles_per_seq):
    t = pl.program_id(0)
    first = (t % tiles_per_seq) == 0
    last = (t % tiles_per_seq) == tiles_per_seq - 1
    xs_ref[0:HALO, :] = jnp.where(first, 0.0, xp_ref[...]).astype(bf16)
    xs_ref[HALO:HALO + ROW_TILE, :] = xm_ref[...].astype(bf16)
    xs_ref[HALO + ROW_TILE:, :] = jnp.where(last, 0.0, xn_ref[...]).astype(bf16)

    def conv(slot, col):
        cw = cw_ref[:, col:col + FF_CHUNK]
        prev = h_ref[slot, HALO - 1:HALO - 1 + ROW_TILE, :]
        cur = h_ref[slot, HALO:HALO + ROW_TILE, :]
        nxt = h_ref[slot, HALO + 1:HALO + 1 + ROW_TILE, :]
        return (cw[0:1, :] * prev + cw[1:2, :] * cur + cw[2:3, :] * nxt
                + cb_ref[:, col:col + FF_CHUNK])

    for c in range(N_FF_CHUNKS):
        sa, su = 2 * (c % 2), 2 * (c % 2) + 1
        col_a, col_u = c * FF_CHUNK, D_FF + c * FF_CHUNK
        h_ref[sa] = _dot(xs_ref[...], wup_ref[:, col_a:col_a + FF_CHUNK])
        h_ref[su] = _dot(xs_ref[...], wup_ref[:, col_u:col_u + FF_CHUNK])
        act = _gelu(conv(sa, col_a)) * conv(su, col_u)
        act_ref[:, col_a:col_a + FF_CHUNK] = act.astype(bf16)
    part = ROW_TILE // OUT_PARTS
    for rows in (slice(r * part, (r + 1) * part) for r in range(OUT_PARTS)):
        ffn = _dot(act_ref[rows, :], wd_ref[...])
        y_ref[rows, :] = _layer_norm(ALPHA * xm_ref[rows, :] + ffn, l2g_ref[...], l2b_ref[...])


def _ffn(x1, seq_len, w_up, conv_w, conv_b, w_down, ln2_g, ln2_b):
    T = x1.shape[0]
    tiles_per_seq = seq_len // ROW_TILE
    hb = ROW_TILE // HALO
    n_hb = T // HALO
    full = lambda *shape: pl.BlockSpec(shape, lambda t: (0,) * len(shape),
                                       pipeline_mode=pl.Buffered(1))
    m = ROW_TILE + 2 * HALO
    return pl.pallas_call(
        functools.partial(_ffn_kernel, tiles_per_seq=tiles_per_seq),
        grid=(T // ROW_TILE,),
        in_specs=[
            pl.BlockSpec((HALO, D_MODEL), lambda t: (jnp.maximum(t * hb - 1, 0), 0)),
            pl.BlockSpec((ROW_TILE, D_MODEL), lambda t: (t, 0)),
            pl.BlockSpec((HALO, D_MODEL), lambda t: (jnp.minimum((t + 1) * hb, n_hb - 1), 0)),
            full(D_MODEL, 2 * D_FF), full(3, 2 * D_FF), full(1, 2 * D_FF),
            full(D_FF, D_MODEL), full(1, D_MODEL), full(1, D_MODEL),
        ],
        out_specs=pl.BlockSpec((ROW_TILE, D_MODEL), lambda t: (t, 0)),
        out_shape=jax.ShapeDtypeStruct((T, D_MODEL), f32),
        scratch_shapes=[
            pltpu.VMEM((m, D_MODEL), bf16),
            pltpu.VMEM((4, m, FF_CHUNK), f32),
            pltpu.VMEM((ROW_TILE, D_FF), bf16),
        ],
        compiler_params=pltpu.CompilerParams(
            dimension_semantics=("arbitrary",), vmem_limit_bytes=VMEM_LIMIT),
        name="ffn",
    )(x1, x1, x1, w_up, conv_w, conv_b, w_down, ln2_g, ln2_b)


def _t5_bucket(rel):
    nb = NUM_BUCKETS // 2
    ret = jnp.where(rel > 0, nb, 0)
    n = jnp.abs(rel)
    max_exact = nb // 2
    nf = jnp.maximum(n, 1).astype(f32)
    large = max_exact + (jnp.log(nf / max_exact) / math.log(MAX_DISTANCE / max_exact)
                         * (nb - max_exact)).astype(jnp.int32)
    large = jnp.minimum(large, nb - 1)
    return ret + jnp.where(n < max_exact, n, large)


def _trig(num, den):
    ang = (num % den).astype(f32) * (2.0 * math.pi / den)
    return jnp.cos(ang), jnp.sin(ang)


def _stage1_tables(seq):
    shape = (RADIX // JB, JB // SUB, RADIX // SUB, 2, SUB, SUB, RADIX)
    jb, hh, kb, part, jj, k1, i = (lax.broadcasted_iota(jnp.int32, shape, d) for d in range(7))
    phase = (kb * SUB + k1) * (RADIX * i + jb * JB + hh * SUB + jj) - part * (seq // 4)
    dense = (_trig(phase, seq)[0] * (seq ** -0.5)).astype(bf16).reshape(-1, RADIX)
    rows, cols = dense.shape[0], RADIX * SUB
    spread = (lax.broadcasted_iota(jnp.int32, (RADIX, cols), 1) // SUB
              == lax.broadcasted_iota(jnp.int32, (RADIX, cols), 0)).astype(bf16)
    wide = jnp.dot(dense, spread, preferred_element_type=bf16)
    keep = ((lax.broadcasted_iota(jnp.int32, (rows, cols), 0) // SUB) % SUB
            == lax.broadcasted_iota(jnp.int32, (rows, cols), 1) % SUB)
    return jnp.where(keep, wide, 0).reshape(RADIX // JB, JB // SUB, 2 * SUB * RADIX, cols)


def _stage2_tables():
    shape = (RADIX * SUB, 2 * RADIX * SUB)
    r, c = (lax.broadcasted_iota(jnp.int32, shape, d) for d in range(2))
    k2, k1 = r // SUB, r % SUB
    part, j, k1_in = c // (RADIX * SUB), (c // SUB) % RADIX, c % SUB
    cos, sin = _trig(j * k2, RADIX)
    same = k1 == k1_in
    ha = jnp.where(same, jnp.where(part == 0, cos, -sin), 0.0)
    hp = jnp.where(same, jnp.where(part == 0, sin, cos), 0.0)
    return ha.astype(bf16), hp.astype(bf16)


def _encode(x, p):
    B, S, _ = x.shape
    assert S == RADIX * RADIX
    T = B * S
    x4 = x.reshape(B, RADIX, RADIX, D_MODEL)
    qkv, y1, res = _in_proj(x4, p["ln_in_g"], p["ln_in_b"], p["w_in"], p["dft1"])
    attn = _attention(qkv.reshape(B, S, QKV_COLS), p["bias"], p["sink"])
    x1 = _mix(y1.reshape(B, RADIX // SUB, 2, RADIX * SUB, FOUR_W), p["ha"], p["hp"], p["cw"],
              res, attn.reshape(B, RADIX, RADIX, ATTN_W),
              p["g_attn"], p["g_four"], p["w_o"], p["ln1_g"], p["ln1_b"])
    y = _ffn(x1.reshape(T, D_MODEL), S, p["w_up"], p["conv_w"], p["conv_b"], p["w_down"], p["ln2_g"], p["ln2_b"])
    return y.reshape(B, S, D_MODEL)


def kernel(x_prompt, x_sample, ln_in_g, ln_in_b, rel_table, w_in, attn_sink, w_fourier, g_attn,
           g_fourier, w_o, ln1_g, ln1_b, w_up, conv_w, conv_b, w_down, ln2_g, ln2_b):
    S = x_prompt.shape[1]
    row = lambda v: v.reshape(1, -1).astype(f32)
    col_scale = jnp.concatenate([jnp.full((ATTN_W,), HEAD_DIM ** -0.5 * LOG2E, f32),
                                 jnp.ones((IN_COLS - ATTN_W,), f32)])
    ha, hp = _stage2_tables()

    def by_pairs(w, axis):
        heads = jnp.split(w, N_HEADS, axis=axis)
        return jnp.concatenate([heads[h] for h in HEAD_ORDER], axis=axis)

    w_in_s = w_in[0] * col_scale
    w_in_p = jnp.concatenate([by_pairs(w_in_s[:, :ATTN_W], 1), w_in_s[:, ATTN_W:]], axis=1)
    w_o_p = jnp.concatenate([by_pairs(w_o[0][:ATTN_W], 0), w_o[0][ATTN_W:]], axis=0)

    p = dict(
        ln_in_g=row(ln_in_g), ln_in_b=row(ln_in_b),
        w_in=w_in_p.astype(bf16),
        bias=_band_bias(rel_table), sink=by_pairs(attn_sink[0].astype(f32), 0) * LOG2E,
        dft1=_stage1_tables(S), ha=ha, hp=hp,
        cw=_fold_channel_dft(w_fourier[0]),
        g_attn=row(by_pairs(g_attn[0], 0)), g_four=row(g_fourier[0]),
        w_o=w_o_p.astype(bf16),
        ln1_g=row(ln1_g[0]), ln1_b=row(ln1_b[0]),
        w_up=w_up[0].astype(bf16), conv_w=conv_w[0].astype(f32), conv_b=row(conv_b[0]),
        w_down=w_down[0].astype(bf16),
        ln2_g=row(ln2_g[0]), ln2_b=row(ln2_b[0]),
    )
    return (_encode(x_prompt, p), _encode(x_sample, p))
```

```python
import functools
import math

import jax
import jax.numpy as jnp
from jax import lax
from jax.experimental import pallas as pl
from jax.experimental.pallas import tpu as pltpu

D_MODEL = 1024
HEAD_DIM = 64
N_HEADS = 8
N_KV_HEADS = 2
GROUP = N_HEADS // N_KV_HEADS
WINDOW = 128
BLOCK = 128
NUM_BUCKETS = 32
MAX_DISTANCE = 128
F_GROUPS = 4
F_CH = 128
ATTN_W = N_HEADS * HEAD_DIM
FOUR_W = F_GROUPS * F_CH
KV_COLS = N_KV_HEADS * HEAD_DIM
QKV_COLS = ATTN_W + 2 * KV_COLS
IN_COLS = QKV_COLS + FOUR_W
D_FF = 2816
EPS = 1e-5
DEPTH = 1
ALPHA = (2.0 * DEPTH) ** 0.25
NEG = -1e30
LOG2E = math.log2(math.e)
N_PAIRS = N_HEADS // 2
HEAD_ORDER = tuple(h for p in range(N_PAIRS) for h in (p, GROUP + p))
Q_BLOCKS = 16

ROW_TILE = 512
OUT_PARTS = 2
FF_CHUNK = 256
N_FF_CHUNKS = D_FF // FF_CHUNK
HALO = 16
RADIX = 64
JB = 16
SUB = 8
VMEM_LIMIT = 56 * 1024 * 1024

bf16 = jnp.bfloat16
f32 = jnp.float32


def _layer_norm(x, g, b):
    mu = jnp.mean(x, axis=-1, keepdims=True)
    xc = x - mu
    var = jnp.mean(xc * xc, axis=-1, keepdims=True)
    return xc * lax.rsqrt(var + EPS) * g + b


def _rms_norm(x, g):
    return x * lax.rsqrt(jnp.mean(x * x, axis=-1, keepdims=True) + EPS) * g


def _dot(a, b):
    return jnp.dot(a, b, preferred_element_type=f32)


def _in_proj_kernel(x_ref, g_ref, b_ref, w_ref, dft_ref, qkv_ref, y_ref, res_ref):
    qkv = []
    for hh in range(JB // SUB):
        jj = slice(hh * SUB, (hh + 1) * SUB)
        x0 = _layer_norm(x_ref[0, :, jj, :].reshape(RADIX * SUB, D_MODEL), g_ref[...], b_ref[...])
        res_ref[0, :, jj, :] = (ALPHA * x0).reshape(RADIX, SUB, D_MODEL)
        h = _dot(x0.astype(bf16), w_ref[...])
        qkv.append(h[:, :QKV_COLS].reshape(RADIX, SUB, QKV_COLS))
        y = _dot(dft_ref[0, hh], h[:, QKV_COLS:].astype(bf16))
        y_ref[0, :, :, 0, hh * SUB * SUB:(hh + 1) * SUB * SUB, :] = (
            y.astype(bf16).reshape(RADIX // SUB, 2, SUB * SUB, FOUR_W))
    qkv_ref[0] = jnp.concatenate(qkv, axis=1).astype(bf16)


def _in_proj(x4, ln_g, ln_b, w_in, dft1):
    B = x4.shape[0]
    return pl.pallas_call(
        _in_proj_kernel,
        grid=(RADIX // JB, B),
        in_specs=[
            pl.BlockSpec((1, RADIX, JB, D_MODEL), lambda jb, b: (b, 0, jb, 0)),
            pl.BlockSpec((1, D_MODEL), lambda jb, b: (0, 0)),
            pl.BlockSpec((1, D_MODEL), lambda jb, b: (0, 0)),
            pl.BlockSpec((D_MODEL, IN_COLS), lambda jb, b: (0, 0)),
            pl.BlockSpec((1, JB // SUB, 2 * RADIX * SUB, RADIX * SUB), lambda jb, b: (jb, 0, 0, 0)),
        ],
        out_specs=[
            pl.BlockSpec((1, RADIX, JB, QKV_COLS), lambda jb, b: (b, 0, jb, 0)),
            pl.BlockSpec((1, RADIX // SUB, 2, 1, JB * SUB, FOUR_W), lambda jb, b: (b, 0, 0, jb, 0, 0)),
            pl.BlockSpec((1, RADIX, JB, D_MODEL), lambda jb, b: (b, 0, jb, 0)),
        ],
        out_shape=[
            jax.ShapeDtypeStruct((B, RADIX, RADIX, QKV_COLS), bf16),
            jax.ShapeDtypeStruct((B, RADIX // SUB, 2, RADIX // JB, JB * SUB, FOUR_W), bf16),
            jax.ShapeDtypeStruct((B, RADIX, RADIX, D_MODEL), f32),
        ],
        compiler_params=pltpu.CompilerParams(
            dimension_semantics=("arbitrary", "arbitrary"), vmem_limit_bytes=VMEM_LIMIT),
        name="in_proj",
    )(x4, ln_g, ln_b, w_in, dft1)


def _attn_kernel(q_ref, k_ref, v_ref, bias_ref, sink_ref, o_ref, *, nblk):
    t = pl.program_id(1)
    seq = nblk * BLOCK
    keys = 3 * BLOCK
    lane = lax.broadcasted_iota(jnp.int32, (keys, KV_COLS), 1)
    low = lane < HEAD_DIM
    out_low = lax.broadcasted_iota(jnp.int32, (BLOCK, KV_COLS), 1) < HEAD_DIM

    def block_diag(ref, starts):
        win = jnp.concatenate([ref[0, pl.ds(s, BLOCK), :] for s in starts], axis=0)
        zero = jnp.zeros_like(win)
        return jnp.concatenate([jnp.where(low, win, zero), jnp.where(low, zero, win)], axis=0)

    def scores(qi):
        i = t * Q_BLOCKS + qi
        starts = [pl.multiple_of(jnp.clip((i + c) * BLOCK, 0, seq - BLOCK), BLOCK) for c in (-1, 0, 1)]
        q = q_ref[0, qi * BLOCK:(qi + 1) * BLOCK, :]
        qs = jnp.concatenate([q[:, p * KV_COLS:(p + 1) * KV_COLS] for p in range(N_PAIRS)], axis=0)
        logits = lax.dot_general(qs, block_diag(k_ref, starts), (((1,), (1,)), ((), ())),
                                 preferred_element_type=f32)
        variant = jnp.where(i == 0, 1, jnp.where(i == nblk - 1, 2, 0))
        return logits + bias_ref[variant], block_diag(v_ref, starts)

    def finish(qi, logits, vbd):
        probs, scales = [], []
        for p in range(N_PAIRS):
            halves, recips = [], []
            for side in range(2):
                l = logits[p * BLOCK:(p + 1) * BLOCK, side * keys:(side + 1) * keys]
                sink = sink_ref[2 * p + side]
                m = jnp.maximum(jnp.max(l, axis=-1, keepdims=True), sink)
                e = jnp.exp2(l - m)
                denom = jnp.sum(e, axis=-1, keepdims=True) + jnp.exp2(sink - m)
                halves.append(e.astype(bf16))
                recips.append(1.0 / denom)
            probs.append(jnp.concatenate(halves, axis=-1))
            scales.append(jnp.where(out_low, recips[0], recips[1]))
        pv = _dot(jnp.concatenate(probs, axis=0), vbd)
        out = [pv[p * BLOCK:(p + 1) * BLOCK, :] * scales[p] for p in range(N_PAIRS)]
        o_ref[0, qi * BLOCK:(qi + 1) * BLOCK, :] = jnp.concatenate(out, axis=-1).astype(bf16)

    staged = scores(0)
    for qi in range(Q_BLOCKS):
        ahead = scores(qi + 1) if qi + 1 < Q_BLOCKS else None
        finish(qi, *staged)
        staged = ahead


def _attention(qkv, bias, sink):
    B, S, _ = qkv.shape
    nblk = S // BLOCK
    assert nblk >= 2 and nblk % Q_BLOCKS == 0
    k_col = ATTN_W // KV_COLS
    rows = Q_BLOCKS * BLOCK
    return pl.pallas_call(
        functools.partial(_attn_kernel, nblk=nblk),
        grid=(B, nblk // Q_BLOCKS),
        in_specs=[
            pl.BlockSpec((1, rows, ATTN_W), lambda b, t: (b, t, 0)),
            pl.BlockSpec((1, S, KV_COLS), lambda b, t: (b, 0, k_col)),
            pl.BlockSpec((1, S, KV_COLS), lambda b, t: (b, 0, k_col + 1)),
            pl.BlockSpec((3, N_PAIRS * BLOCK, 6 * BLOCK), lambda b, t: (0, 0, 0)),
            pl.BlockSpec(memory_space=pltpu.SMEM),
        ],
        out_specs=pl.BlockSpec((1, rows, ATTN_W), lambda b, t: (b, t, 0)),
        out_shape=jax.ShapeDtypeStruct((B, S, ATTN_W), bf16),
        compiler_params=pltpu.CompilerParams(
            dimension_semantics=("arbitrary", "arbitrary"), vmem_limit_bytes=VMEM_LIMIT),
        name="attn",
    )(qkv, qkv, qkv, bias, sink)


def _bias_kernel(bucket_ref, rel_ref, table_ref, o_ref):
    bucket = bucket_ref[...]
    in_window = jnp.abs(rel_ref[...]) <= WINDOW
    for h in range(N_HEADS):
        acc = jnp.zeros(bucket.shape, f32)
        for b in range(NUM_BUCKETS):
            acc = jnp.where(bucket == b, table_ref[b, h], acc)
        o_ref[h] = jnp.where(in_window, acc * LOG2E, NEG)


def _band_bias(rel_table):
    rel = (jnp.arange(3 * BLOCK, dtype=jnp.int32)[None, :] - BLOCK) - jnp.arange(BLOCK, dtype=jnp.int32)[:, None]
    bias = pl.pallas_call(
        _bias_kernel,
        in_specs=[pl.BlockSpec(rel.shape, lambda: (0, 0)), pl.BlockSpec(rel.shape, lambda: (0, 0)),
                  pl.BlockSpec(memory_space=pltpu.SMEM)],
        out_specs=pl.BlockSpec((N_HEADS,) + rel.shape, lambda: (0, 0, 0)),
        out_shape=jax.ShapeDtypeStruct((N_HEADS,) + rel.shape, f32),
        name="bias_table",
    )(_t5_bucket(rel), rel, rel_table.astype(f32))
    chunk = jnp.arange(3 * BLOCK, dtype=jnp.int32) // BLOCK
    variants = jnp.stack([bias, jnp.where(chunk == 0, NEG, bias), jnp.where(chunk == 2, NEG, bias)])
    pairs = jnp.stack([variants[:, h] for h in HEAD_ORDER], axis=1)
    pairs = pairs.reshape(3, N_PAIRS, 2, BLOCK, 3 * BLOCK)
    return pairs.transpose(0, 1, 3, 2, 4).reshape(3, N_PAIRS * BLOCK, 6 * BLOCK)


def _fourier_rows(ys, ha_ref, hp_ref, cw_ref):
    a = _dot(ha_ref[...], ys).astype(bf16)
    p = _dot(hp_ref[...], ys).astype(bf16)
    four = []
    for g in range(F_GROUPS):
        cols = slice(g * F_CH, (g + 1) * F_CH)
        ap = jnp.concatenate([a[:, cols], p[:, cols]], axis=-1)
        four.append(_dot(ap, cw_ref[g]))
    return jnp.concatenate(four, axis=-1)


def _fold_kernel(cs_ref, wf_ref, o_ref):
    for g in range(F_GROUPS):
        o_ref[g] = jnp.dot(cs_ref[...], wf_ref[g], preferred_element_type=f32,
                           precision=lax.Precision.HIGHEST).astype(bf16)


def _fold_channel_dft(w_f):
    ar = jnp.arange(F_CH, dtype=jnp.int32)
    cc, sc = _trig(ar[:, None] * ar[None, :], F_CH)
    cs = jnp.concatenate([cc, -sc], axis=0) * F_CH ** -0.5
    return pl.pallas_call(
        _fold_kernel,
        out_shape=jax.ShapeDtypeStruct((F_GROUPS, 2 * F_CH, F_CH), bf16),
        name="fold_channel_dft",
    )(cs, w_f.astype(f32))


def _mix_kernel(y_ref, ha_ref, hp_ref, cw_ref, res_ref, attn_ref, ga_ref, gf_ref,
                wo_ref, l1g_ref, l1b_ref, x1_ref):
    rows = RADIX * SUB
    attn = attn_ref[0].astype(f32)
    for hh in range(JB // SUB):
        kk = slice(hh * SUB, (hh + 1) * SUB)
        ys = y_ref[0, hh].reshape(2 * rows, FOUR_W)
        fn = _rms_norm(_fourier_rows(ys, ha_ref, hp_ref, cw_ref), gf_ref[...]).astype(bf16)
        an = _rms_norm(attn[:, kk, :].reshape(rows, ATTN_W), ga_ref[...]).astype(bf16)
        mix = _dot(an, wo_ref[:ATTN_W, :]) + _dot(fn, wo_ref[ATTN_W:, :])
        x1 = _layer_norm(res_ref[0, :, kk, :].reshape(rows, D_MODEL) + mix, l1g_ref[...], l1b_ref[...])
        x1_ref[0, :, kk, :] = x1.reshape(RADIX, SUB, D_MODEL)


def _mix(y, ha, hp, cw, res, attn, g_attn, g_four, w_o, ln1_g, ln1_b):
    B = y.shape[0]
    groups = JB // SUB
    full = lambda *shape: pl.BlockSpec(shape, lambda b, kb: (0,) * len(shape))
    tile = lambda w: pl.BlockSpec((1, RADIX, JB, w), lambda b, kb: (b, 0, kb, 0))
    return pl.pallas_call(
        _mix_kernel,
        grid=(B, RADIX // JB),
        in_specs=[
            pl.BlockSpec((1, groups, 2, RADIX * SUB, FOUR_W), lambda b, kb: (b, kb, 0, 0, 0)),
            full(RADIX * SUB, 2 * RADIX * SUB), full(RADIX * SUB, 2 * RADIX * SUB),
            full(F_GROUPS, 2 * F_CH, F_CH),
            tile(D_MODEL), tile(ATTN_W),
            full(1, ATTN_W), full(1, FOUR_W), full(D_MODEL, D_MODEL),
            full(1, D_MODEL), full(1, D_MODEL),
        ],
        out_specs=tile(D_MODEL),
        out_shape=jax.ShapeDtypeStruct((B, RADIX, RADIX, D_MODEL), f32),
        compiler_params=pltpu.CompilerParams(
            dimension_semantics=("arbitrary", "arbitrary"), vmem_limit_bytes=VMEM_LIMIT),
        name="mix",
    )(y, ha, hp, cw, res, attn, g_attn, g_four, w_o, ln1_g, ln1_b)


def _gelu(x):
    return 0.5 * x * (1.0 + lax.erf(x * (1.0 / math.sqrt(2.0))))


def _ffn_kernel(xp_ref, xm_ref, xn_ref, wup_ref, cw_ref, cb_ref, wd_ref, l2g_ref, l2b_ref,
                y_ref, xs_ref, h_ref, act_ref, *, tiles_per_seq):
    t = pl.program_id(0)
    first = (t % tiles_per_seq) == 0
    last = (t % tiles_per_seq) == tiles_per_seq - 1
    xs_ref[0:HALO, :] = jnp.where(first, 0.0, xp_ref[...]).astype(bf16)
    xs_ref[HALO:HALO + ROW_TILE, :] = xm_ref[...].astype(bf16)
    xs_ref[HALO + ROW_TILE:, :] = jnp.where(last, 0.0, xn_ref[...]).astype(bf16)

    def conv(slot, col):
        cw = cw_ref[:, col:col + FF_CHUNK]
        wide = h_ref[slot, HALO - 8:HALO + ROW_TILE + 8, :]
        n = ROW_TILE + 16
        prev = pltpu.roll(wide, 1, 0)[8:8 + ROW_TILE]
        nxt = pltpu.roll(wide, n - 1, 0)[8:8 + ROW_TILE]
        return (cw[0:1, :] * prev + cw[1:2, :] * wide[8:8 + ROW_TILE] + cw[2:3, :] * nxt
                + cb_ref[:, col:col + FF_CHUNK])

    def up(c):
        col_a, col_u = c * FF_CHUNK, D_FF + c * FF_CHUNK
        h_ref[2 * (c % 2)] = _dot(xs_ref[...], wup_ref[:, col_a:col_a + FF_CHUNK])
        h_ref[2 * (c % 2) + 1] = _dot(xs_ref[...], wup_ref[:, col_u:col_u + FF_CHUNK])

    up(0)
    for c in range(N_FF_CHUNKS):
        if c + 1 < N_FF_CHUNKS:
            up(c + 1)
        col_a, col_u = c * FF_CHUNK, D_FF + c * FF_CHUNK
        act = _gelu(conv(2 * (c % 2), col_a)) * conv(2 * (c % 2) + 1, col_u)
        act_ref[:, col_a:col_a + FF_CHUNK] = act.astype(bf16)
    part = ROW_TILE // OUT_PARTS
    for rows in (slice(r * part, (r + 1) * part) for r in range(OUT_PARTS)):
        ffn = _dot(act_ref[rows, :], wd_ref[...])
        y_ref[rows, :] = _layer_norm(ALPHA * xm_ref[rows, :] + ffn, l2g_ref[...], l2b_ref[...])


def _ffn(x1, seq_len, w_up, conv_w, conv_b, w_down, ln2_g, ln2_b):
    T = x1.shape[0]
    tiles_per_seq = seq_len // ROW_TILE
    hb = ROW_TILE // HALO
    n_hb = T // HALO
    full = lambda *shape: pl.BlockSpec(shape, lambda t: (0,) * len(shape),
                                       pipeline_mode=pl.Buffered(1))
    m = ROW_TILE + 2 * HALO
    return pl.pallas_call(
        functools.partial(_ffn_kernel, tiles_per_seq=tiles_per_seq),
        grid=(T // ROW_TILE,),
        in_specs=[
            pl.BlockSpec((HALO, D_MODEL), lambda t: (jnp.maximum(t * hb - 1, 0), 0)),
            pl.BlockSpec((ROW_TILE, D_MODEL), lambda t: (t, 0)),
            pl.BlockSpec((HALO, D_MODEL), lambda t: (jnp.minimum((t + 1) * hb, n_hb - 1), 0)),
            full(D_MODEL, 2 * D_FF), full(3, 2 * D_FF), full(1, 2 * D_FF),
            full(D_FF, D_MODEL), full(1, D_MODEL), full(1, D_MODEL),
        ],
        out_specs=pl.BlockSpec((ROW_TILE, D_MODEL), lambda t: (t, 0)),
        out_shape=jax.ShapeDtypeStruct((T, D_MODEL), f32),
        scratch_shapes=[
            pltpu.VMEM((m, D_MODEL), bf16),
            pltpu.VMEM((4, m, FF_CHUNK), f32),
            pltpu.VMEM((ROW_TILE, D_FF), bf16),
        ],
        compiler_params=pltpu.CompilerParams(
            dimension_semantics=("arbitrary",), vmem_limit_bytes=VMEM_LIMIT),
        name="ffn",
    )(x1, x1, x1, w_up, conv_w, conv_b, w_down, ln2_g, ln2_b)


def _t5_bucket(rel):
    nb = NUM_BUCKETS // 2
    ret = jnp.where(rel > 0, nb, 0)
    n = jnp.abs(rel)
    max_exact = nb // 2
    nf = jnp.maximum(n, 1).astype(f32)
    large = max_exact + (jnp.log(nf / max_exact) / math.log(MAX_DISTANCE / max_exact)
                         * (nb - max_exact)).astype(jnp.int32)
    large = jnp.minimum(large, nb - 1)
    return ret + jnp.where(n < max_exact, n, large)


def _trig(num, den):
    ang = (num % den).astype(f32) * (2.0 * math.pi / den)
    return jnp.cos(ang), jnp.sin(ang)


def _stage1_tables(seq):
    shape = (RADIX // JB, JB // SUB, RADIX // SUB, 2, SUB, SUB, RADIX)
    jb, hh, kb, part, jj, k1, i = (lax.broadcasted_iota(jnp.int32, shape, d) for d in range(7))
    phase = (kb * SUB + k1) * (RADIX * i + jb * JB + hh * SUB + jj) - part * (seq // 4)
    dense = (_trig(phase, seq)[0] * (seq ** -0.5)).astype(bf16).reshape(-1, RADIX)
    rows, cols = dense.shape[0], RADIX * SUB
    spread = (lax.broadcasted_iota(jnp.int32, (RADIX, cols), 1) // SUB
              == lax.broadcasted_iota(jnp.int32, (RADIX, cols), 0)).astype(bf16)
    wide = jnp.dot(dense, spread, preferred_element_type=bf16)
    keep = ((lax.broadcasted_iota(jnp.int32, (rows, cols), 0) // SUB) % SUB
            == lax.broadcasted_iota(jnp.int32, (rows, cols), 1) % SUB)
    return jnp.where(keep, wide, 0).reshape(RADIX // JB, JB // SUB, 2 * SUB * RADIX, cols)


def _stage2_tables():
    shape = (RADIX * SUB, 2 * RADIX * SUB)
    r, c = (lax.broadcasted_iota(jnp.int32, shape, d) for d in range(2))
    k2, k1 = r // SUB, r % SUB
    part, j, k1_in = c // (RADIX * SUB), (c // SUB) % RADIX, c % SUB
    cos, sin = _trig(j * k2, RADIX)
    same = k1 == k1_in
    ha = jnp.where(same, jnp.where(part == 0, cos, -sin), 0.0)
    hp = jnp.where(same, jnp.where(part == 0, sin, cos), 0.0)
    return ha.astype(bf16), hp.astype(bf16)


def _encode(x, p):
    B, S, _ = x.shape
    assert S == RADIX * RADIX
    T = B * S
    x4 = x.reshape(B, RADIX, RADIX, D_MODEL)
    qkv, y1, res = _in_proj(x4, p["ln_in_g"], p["ln_in_b"], p["w_in"], p["dft1"])
    attn = _attention(qkv.reshape(B, S, QKV_COLS), p["bias"], p["sink"])
    x1 = _mix(y1.reshape(B, RADIX // SUB, 2, RADIX * SUB, FOUR_W), p["ha"], p["hp"], p["cw"],
              res, attn.reshape(B, RADIX, RADIX, ATTN_W),
              p["g_attn"], p["g_four"], p["w_o"], p["ln1_g"], p["ln1_b"])
    y = _ffn(x1.reshape(T, D_MODEL), S, p["w_up"], p["conv_w"], p["conv_b"], p["w_down"], p["ln2_g"], p["ln2_b"])
    return y.reshape(B, S, D_MODEL)


def kernel(x_prompt, x_sample, ln_in_g, ln_in_b, rel_table, w_in, attn_sink, w_fourier, g_attn,
           g_fourier, w_o, ln1_g, ln1_b, w_up, conv_w, conv_b, w_down, ln2_g, ln2_b):
    S = x_prompt.shape[1]
    row = lambda v: v.reshape(1, -1).astype(f32)
    col_scale = jnp.concatenate([jnp.full((ATTN_W,), HEAD_DIM ** -0.5 * LOG2E, f32),
                                 jnp.ones((IN_COLS - ATTN_W,), f32)])
    ha, hp = _stage2_tables()

    def by_pairs(w, axis):
        heads = jnp.split(w, N_HEADS, axis=axis)
        return jnp.concatenate([heads[h] for h in HEAD_ORDER], axis=axis)

    w_in_s = w_in[0] * col_scale
    w_in_p = jnp.concatenate([by_pairs(w_in_s[:, :ATTN_W], 1), w_in_s[:, ATTN_W:]], axis=1)
    w_o_p = jnp.concatenate([by_pairs(w_o[0][:ATTN_W], 0), w_o[0][ATTN_W:]], axis=0)

    p = dict(
        ln_in_g=row(ln_in_g), ln_in_b=row(ln_in_b),
        w_in=w_in_p.astype(bf16),
        bias=_band_bias(rel_table), sink=by_pairs(attn_sink[0].astype(f32), 0) * LOG2E,
        dft1=_stage1_tables(S), ha=ha, hp=hp,
        cw=_fold_channel_dft(w_fourier[0]),
        g_attn=row(by_pairs(g_attn[0], 0)), g_four=row(g_fourier[0]),
        w_o=w_o_p.astype(bf16),
        ln1_g=row(ln1_g[0]), ln1_b=row(ln1_b[0]),
        w_up=w_up[0].astype(bf16), conv_w=conv_w[0].astype(f32), conv_b=row(conv_b[0]),
        w_down=w_down[0].astype(bf16),
        ln2_g=row(ln2_g[0]), ln2_b=row(ln2_b[0]),
    )
    return (_encode(x_prompt, p), _encode(x_sample, p))
```

```python
import functools
import math

import jax
import jax.numpy as jnp
from jax import lax
from jax.experimental import pallas as pl
from jax.experimental.pallas import tpu as pltpu

D_MODEL = 1024
HEAD_DIM = 64
N_HEADS = 8
N_KV_HEADS = 2
GROUP = N_HEADS // N_KV_HEADS
WINDOW = 128
BLOCK = 128
NUM_BUCKETS = 32
MAX_DISTANCE = 128
F_GROUPS = 4
F_CH = 128
ATTN_W = N_HEADS * HEAD_DIM
FOUR_W = F_GROUPS * F_CH
KV_COLS = N_KV_HEADS * HEAD_DIM
QKV_COLS = ATTN_W + 2 * KV_COLS
IN_COLS = QKV_COLS + FOUR_W
D_FF = 2816
EPS = 1e-5
DEPTH = 1
ALPHA = (2.0 * DEPTH) ** 0.25
NEG = -1e30
LOG2E = math.log2(math.e)
N_PAIRS = N_HEADS // 2
HEAD_ORDER = tuple(h for p in range(N_PAIRS) for h in (p, GROUP + p))
Q_BLOCKS = 16

ROW_TILE = 512
OUT_PARTS = 2
FF_CHUNK = 256
N_FF_CHUNKS = D_FF // FF_CHUNK
HALO = 8
RADIX = 64
JB = 16
SUB = 8
VMEM_LIMIT = 56 * 1024 * 1024

bf16 = jnp.bfloat16
f32 = jnp.float32


def _layer_norm(x, g, b):
    mu = jnp.mean(x, axis=-1, keepdims=True)
    xc = x - mu
    var = jnp.mean(xc * xc, axis=-1, keepdims=True)
    return xc * lax.rsqrt(var + EPS) * g + b


def _rms_norm(x, g):
    return x * lax.rsqrt(jnp.mean(x * x, axis=-1, keepdims=True) + EPS) * g


def _dot(a, b):
    return jnp.dot(a, b, preferred_element_type=f32)


def _in_proj_kernel(x_ref, g_ref, b_ref, w_ref, dft_ref, qkv_ref, y_ref, res_ref):
    qkv = []
    for hh in range(JB // SUB):
        jj = slice(hh * SUB, (hh + 1) * SUB)
        x0 = _layer_norm(x_ref[0, :, jj, :].reshape(RADIX * SUB, D_MODEL), g_ref[...], b_ref[...])
        res_ref[0, :, jj, :] = (ALPHA * x0).reshape(RADIX, SUB, D_MODEL)
        h = _dot(x0.astype(bf16), w_ref[...])
        qkv.append(h[:, :QKV_COLS].reshape(RADIX, SUB, QKV_COLS))
        y = _dot(dft_ref[0, hh], h[:, QKV_COLS:].astype(bf16))
        y_ref[0, :, :, 0, hh * SUB * SUB:(hh + 1) * SUB * SUB, :] = (
            y.astype(bf16).reshape(RADIX // SUB, 2, SUB * SUB, FOUR_W))
    qkv_ref[0] = jnp.concatenate(qkv, axis=1).astype(bf16)


def _in_proj(x4, ln_g, ln_b, w_in, dft1):
    B = x4.shape[0]
    return pl.pallas_call(
        _in_proj_kernel,
        grid=(RADIX // JB, B),
        in_specs=[
            pl.BlockSpec((1, RADIX, JB, D_MODEL), lambda jb, b: (b, 0, jb, 0)),
            pl.BlockSpec((1, D_MODEL), lambda jb, b: (0, 0)),
            pl.BlockSpec((1, D_MODEL), lambda jb, b: (0, 0)),
            pl.BlockSpec((D_MODEL, IN_COLS), lambda jb, b: (0, 0)),
            pl.BlockSpec((1, JB // SUB, 2 * RADIX * SUB, RADIX * SUB), lambda jb, b: (jb, 0, 0, 0)),
        ],
        out_specs=[
            pl.BlockSpec((1, RADIX, JB, QKV_COLS), lambda jb, b: (b, 0, jb, 0)),
            pl.BlockSpec((1, RADIX // SUB, 2, 1, JB * SUB, FOUR_W), lambda jb, b: (b, 0, 0, jb, 0, 0)),
            pl.BlockSpec((1, RADIX, JB, D_MODEL), lambda jb, b: (b, 0, jb, 0)),
        ],
        out_shape=[
            jax.ShapeDtypeStruct((B, RADIX, RADIX, QKV_COLS), bf16),
            jax.ShapeDtypeStruct((B, RADIX // SUB, 2, RADIX // JB, JB * SUB, FOUR_W), bf16),
            jax.ShapeDtypeStruct((B, RADIX, RADIX, D_MODEL), f32),
        ],
        compiler_params=pltpu.CompilerParams(
            dimension_semantics=("arbitrary", "arbitrary"), vmem_limit_bytes=VMEM_LIMIT),
        name="in_proj",
    )(x4, ln_g, ln_b, w_in, dft1)


def _attn_kernel(q_ref, k_ref, v_ref, bias_ref, sink_ref, o_ref, *, nblk):
    t = pl.program_id(1)
    seq = nblk * BLOCK
    keys = 3 * BLOCK
    lane = lax.broadcasted_iota(jnp.int32, (keys, KV_COLS), 1)
    low = lane < HEAD_DIM
    out_low = lax.broadcasted_iota(jnp.int32, (BLOCK, KV_COLS), 1) < HEAD_DIM

    def block_diag(ref, starts):
        win = jnp.concatenate([ref[0, pl.ds(s, BLOCK), :] for s in starts], axis=0)
        zero = jnp.zeros_like(win)
        return jnp.concatenate([jnp.where(low, win, zero), jnp.where(low, zero, win)], axis=0)

    def scores(qi):
        i = t * Q_BLOCKS + qi
        starts = [pl.multiple_of(jnp.clip((i + c) * BLOCK, 0, seq - BLOCK), BLOCK) for c in (-1, 0, 1)]
        q = q_ref[0, qi * BLOCK:(qi + 1) * BLOCK, :]
        qs = jnp.concatenate([q[:, p * KV_COLS:(p + 1) * KV_COLS] for p in range(N_PAIRS)], axis=0)
        logits = lax.dot_general(qs, block_diag(k_ref, starts), (((1,), (1,)), ((), ())),
                                 preferred_element_type=f32)
        variant = jnp.where(i == 0, 1, jnp.where(i == nblk - 1, 2, 0))
        return logits + bias_ref[variant], block_diag(v_ref, starts)

    def finish(qi, logits, vbd):
        probs, scales = [], []
        for p in range(N_PAIRS):
            halves, recips = [], []
            for side in range(2):
                l = logits[p * BLOCK:(p + 1) * BLOCK, side * keys:(side + 1) * keys]
                sink = sink_ref[2 * p + side]
                m = jnp.maximum(jnp.max(l, axis=-1, keepdims=True), sink)
                e = jnp.exp2(l - m)
                denom = jnp.sum(e, axis=-1, keepdims=True) + jnp.exp2(sink - m)
                halves.append(e.astype(bf16))
                recips.append(1.0 / denom)
            probs.append(jnp.concatenate(halves, axis=-1))
            scales.append(jnp.where(out_low, recips[0], recips[1]))
        pv = _dot(jnp.concatenate(probs, axis=0), vbd)
        out = [pv[p * BLOCK:(p + 1) * BLOCK, :] * scales[p] for p in range(N_PAIRS)]
        o_ref[0, qi * BLOCK:(qi + 1) * BLOCK, :] = jnp.concatenate(out, axis=-1).astype(bf16)

    staged = scores(0)
    for qi in range(Q_BLOCKS):
        ahead = scores(qi + 1) if qi + 1 < Q_BLOCKS else None
        finish(qi, *staged)
        staged = ahead


def _attention(qkv, bias, sink):
    B, S, _ = qkv.shape
    nblk = S // BLOCK
    assert nblk >= 2 and nblk % Q_BLOCKS == 0
    k_col = ATTN_W // KV_COLS
    rows = Q_BLOCKS * BLOCK
    return pl.pallas_call(
        functools.partial(_attn_kernel, nblk=nblk),
        grid=(B, nblk // Q_BLOCKS),
        in_specs=[
            pl.BlockSpec((1, rows, ATTN_W), lambda b, t: (b, t, 0)),
            pl.BlockSpec((1, S, KV_COLS), lambda b, t: (b, 0, k_col)),
            pl.BlockSpec((1, S, KV_COLS), lambda b, t: (b, 0, k_col + 1)),
            pl.BlockSpec((3, N_PAIRS * BLOCK, 6 * BLOCK), lambda b, t: (0, 0, 0)),
            pl.BlockSpec(memory_space=pltpu.SMEM),
        ],
        out_specs=pl.BlockSpec((1, rows, ATTN_W), lambda b, t: (b, t, 0)),
        out_shape=jax.ShapeDtypeStruct((B, S, ATTN_W), bf16),
        compiler_params=pltpu.CompilerParams(
            dimension_semantics=("arbitrary", "arbitrary"), vmem_limit_bytes=VMEM_LIMIT),
        name="attn",
    )(qkv, qkv, qkv, bias, sink)


def _bias_kernel(bucket_ref, rel_ref, table_ref, o_ref):
    bucket = bucket_ref[...]
    in_window = jnp.abs(rel_ref[...]) <= WINDOW
    for h in range(N_HEADS):
        acc = jnp.zeros(bucket.shape, f32)
        for b in range(NUM_BUCKETS):
            acc = jnp.where(bucket == b, table_ref[b, h], acc)
        o_ref[h] = jnp.where(in_window, acc * LOG2E, NEG)


def _band_bias(rel_table):
    rel = (jnp.arange(3 * BLOCK, dtype=jnp.int32)[None, :] - BLOCK) - jnp.arange(BLOCK, dtype=jnp.int32)[:, None]
    bias = pl.pallas_call(
        _bias_kernel,
        in_specs=[pl.BlockSpec(rel.shape, lambda: (0, 0)), pl.BlockSpec(rel.shape, lambda: (0, 0)),
                  pl.BlockSpec(memory_space=pltpu.SMEM)],
        out_specs=pl.BlockSpec((N_HEADS,) + rel.shape, lambda: (0, 0, 0)),
        out_shape=jax.ShapeDtypeStruct((N_HEADS,) + rel.shape, f32),
        name="bias_table",
    )(_t5_bucket(rel), rel, rel_table.astype(f32))
    chunk = jnp.arange(3 * BLOCK, dtype=jnp.int32) // BLOCK
    variants = jnp.stack([bias, jnp.where(chunk == 0, NEG, bias), jnp.where(chunk == 2, NEG, bias)])
    pairs = jnp.stack([variants[:, h] for h in HEAD_ORDER], axis=1)
    pairs = pairs.reshape(3, N_PAIRS, 2, BLOCK, 3 * BLOCK)
    return pairs.transpose(0, 1, 3, 2, 4).reshape(3, N_PAIRS * BLOCK, 6 * BLOCK)


def _fourier_rows(ys, ha_ref, hp_ref, cw_ref):
    a = _dot(ha_ref[...], ys).astype(bf16)
    p = _dot(hp_ref[...], ys).astype(bf16)
    four = []
    for g in range(F_GROUPS):
        cols = slice(g * F_CH, (g + 1) * F_CH)
        ap = jnp.concatenate([a[:, cols], p[:, cols]], axis=-1)
        four.append(_dot(ap, cw_ref[g]))
    return jnp.concatenate(four, axis=-1)


def _fold_kernel(cs_ref, wf_ref, o_ref):
    for g in range(F_GROUPS):
        o_ref[g] = jnp.dot(cs_ref[...], wf_ref[g], preferred_element_type=f32,
                           precision=lax.Precision.HIGHEST).astype(bf16)


def _fold_channel_dft(w_f):
    ar = jnp.arange(F_CH, dtype=jnp.int32)
    cc, sc = _trig(ar[:, None] * ar[None, :], F_CH)
    cs = jnp.concatenate([cc, -sc], axis=0) * F_CH ** -0.5
    return pl.pallas_call(
        _fold_kernel,
        out_shape=jax.ShapeDtypeStruct((F_GROUPS, 2 * F_CH, F_CH), bf16),
        name="fold_channel_dft",
    )(cs, w_f.astype(f32))


def _mix_kernel(y_ref, ha_ref, hp_ref, cw_ref, res_ref, attn_ref, ga_ref, gf_ref,
                wo_ref, l1g_ref, l1b_ref, x1_ref):
    rows = RADIX * SUB
    attn = attn_ref[0].astype(f32)
    for hh in range(JB // SUB):
        kk = slice(hh * SUB, (hh + 1) * SUB)
        ys = y_ref[0, hh].reshape(2 * rows, FOUR_W)
        fn = _rms_norm(_fourier_rows(ys, ha_ref, hp_ref, cw_ref), gf_ref[...]).astype(bf16)
        an = _rms_norm(attn[:, kk, :].reshape(rows, ATTN_W), ga_ref[...]).astype(bf16)
        mix = _dot(an, wo_ref[:ATTN_W, :]) + _dot(fn, wo_ref[ATTN_W:, :])
        x1 = _layer_norm(res_ref[0, :, kk, :].reshape(rows, D_MODEL) + mix, l1g_ref[...], l1b_ref[...])
        x1_ref[0, :, kk, :] = x1.reshape(RADIX, SUB, D_MODEL)


def _mix(y, ha, hp, cw, res, attn, g_attn, g_four, w_o, ln1_g, ln1_b):
    B = y.shape[0]
    groups = JB // SUB
    full = lambda *shape: pl.BlockSpec(shape, lambda b, kb: (0,) * len(shape))
    tile = lambda w: pl.BlockSpec((1, RADIX, JB, w), lambda b, kb: (b, 0, kb, 0))
    return pl.pallas_call(
        _mix_kernel,
        grid=(B, RADIX // JB),
        in_specs=[
            pl.BlockSpec((1, groups, 2, RADIX * SUB, FOUR_W), lambda b, kb: (b, kb, 0, 0, 0)),
            full(RADIX * SUB, 2 * RADIX * SUB), full(RADIX * SUB, 2 * RADIX * SUB),
            full(F_GROUPS, 2 * F_CH, F_CH),
            tile(D_MODEL), tile(ATTN_W),
            full(1, ATTN_W), full(1, FOUR_W), full(D_MODEL, D_MODEL),
            full(1, D_MODEL), full(1, D_MODEL),
        ],
        out_specs=tile(D_MODEL),
        out_shape=jax.ShapeDtypeStruct((B, RADIX, RADIX, D_MODEL), f32),
        compiler_params=pltpu.CompilerParams(
            dimension_semantics=("arbitrary", "arbitrary"), vmem_limit_bytes=VMEM_LIMIT),
        name="mix",
    )(y, ha, hp, cw, res, attn, g_attn, g_four, w_o, ln1_g, ln1_b)


def _gelu2(x):
    return x * (1.0 + lax.erf(x * (1.0 / math.sqrt(2.0))))


def _ffn_kernel(xp_ref, xm_ref, xn_ref, wup_ref, cw_ref, cb_ref, wd_ref, l2g_ref, l2b_ref,
                y_ref, xs_ref, h_ref, act_ref, *, tiles_per_seq):
    t = pl.program_id(0)
    first = (t % tiles_per_seq) == 0
    last = (t % tiles_per_seq) == tiles_per_seq - 1
    xs_ref[...] = jnp.concatenate([jnp.where(first, 0.0, xp_ref[...]), xm_ref[...],
                                   jnp.where(last, 0.0, xn_ref[...])], axis=0).astype(bf16)

    def conv(c):
        cols = slice(2 * c * FF_CHUNK, 2 * (c + 1) * FF_CHUNK)
        cw = cw_ref[:, cols]
        wide = h_ref[c % 2]
        n = ROW_TILE + 2 * HALO
        prev = pltpu.roll(wide, 1, 0)[HALO:HALO + ROW_TILE]
        nxt = pltpu.roll(wide, n - 1, 0)[HALO:HALO + ROW_TILE]
        return (cw[0:1, :] * prev + cw[1:2, :] * wide[HALO:HALO + ROW_TILE] + cw[2:3, :] * nxt
                + cb_ref[:, cols])

    def up(c):
        h_ref[c % 2] = _dot(xs_ref[...], wup_ref[:, 2 * c * FF_CHUNK:2 * (c + 1) * FF_CHUNK])

    up(0)
    for c in range(N_FF_CHUNKS):
        if c + 1 < N_FF_CHUNKS:
            up(c + 1)
        au = conv(c)
        act = _gelu2(au[:, :FF_CHUNK]) * au[:, FF_CHUNK:]
        act_ref[:, c * FF_CHUNK:(c + 1) * FF_CHUNK] = act.astype(bf16)
    part = ROW_TILE // OUT_PARTS
    for rows in (slice(r * part, (r + 1) * part) for r in range(OUT_PARTS)):
        ffn = _dot(act_ref[rows, :], wd_ref[...])
        y_ref[rows, :] = _layer_norm(ALPHA * xm_ref[rows, :] + ffn, l2g_ref[...], l2b_ref[...])


def _ffn(x1, seq_len, w_up, conv_w, conv_b, w_down, ln2_g, ln2_b):
    T = x1.shape[0]
    tiles_per_seq = seq_len // ROW_TILE
    hb = ROW_TILE // HALO
    n_hb = T // HALO
    full = lambda *shape: pl.BlockSpec(shape, lambda t: (0,) * len(shape),
                                       pipeline_mode=pl.Buffered(1))
    m = ROW_TILE + 2 * HALO
    return pl.pallas_call(
        functools.partial(_ffn_kernel, tiles_per_seq=tiles_per_seq),
        grid=(T // ROW_TILE,),
        in_specs=[
            pl.BlockSpec((HALO, D_MODEL), lambda t: (jnp.maximum(t * hb - 1, 0), 0)),
            pl.BlockSpec((ROW_TILE, D_MODEL), lambda t: (t, 0)),
            pl.BlockSpec((HALO, D_MODEL), lambda t: (jnp.minimum((t + 1) * hb, n_hb - 1), 0)),
            full(D_MODEL, 2 * D_FF), full(3, 2 * D_FF), full(1, 2 * D_FF),
            full(D_FF, D_MODEL), full(1, D_MODEL), full(1, D_MODEL),
        ],
        out_specs=pl.BlockSpec((ROW_TILE, D_MODEL), lambda t: (t, 0)),
        out_shape=jax.ShapeDtypeStruct((T, D_MODEL), f32),
        scratch_shapes=[
            pltpu.VMEM((m, D_MODEL), bf16),
            pltpu.VMEM((2, m, 2 * FF_CHUNK), f32),
            pltpu.VMEM((ROW_TILE, D_FF), bf16),
        ],
        compiler_params=pltpu.CompilerParams(
            dimension_semantics=("arbitrary",), vmem_limit_bytes=VMEM_LIMIT),
        name="ffn",
    )(x1, x1, x1, w_up, conv_w, conv_b, w_down, ln2_g, ln2_b)


def _t5_bucket(rel):
    nb = NUM_BUCKETS // 2
    ret = jnp.where(rel > 0, nb, 0)
    n = jnp.abs(rel)
    max_exact = nb // 2
    nf = jnp.maximum(n, 1).astype(f32)
    large = max_exact + (jnp.log(nf / max_exact) / math.log(MAX_DISTANCE / max_exact)
                         * (nb - max_exact)).astype(jnp.int32)
    large = jnp.minimum(large, nb - 1)
    return ret + jnp.where(n < max_exact, n, large)


def _trig(num, den):
    ang = (num % den).astype(f32) * (2.0 * math.pi / den)
    return jnp.cos(ang), jnp.sin(ang)


def _stage1_tables(seq):
    shape = (RADIX // JB, JB // SUB, RADIX // SUB, 2, SUB, SUB, RADIX)
    jb, hh, kb, part, jj, k1, i = (lax.broadcasted_iota(jnp.int32, shape, d) for d in range(7))
    phase = (kb * SUB + k1) * (RADIX * i + jb * JB + hh * SUB + jj) - part * (seq // 4)
    dense = (_trig(phase, seq)[0] * (seq ** -0.5)).astype(bf16).reshape(-1, RADIX)
    rows, cols = dense.shape[0], RADIX * SUB
    spread = (lax.broadcasted_iota(jnp.int32, (RADIX, cols), 1) // SUB
              == lax.broadcasted_iota(jnp.int32, (RADIX, cols), 0)).astype(bf16)
    wide = jnp.dot(dense, spread, preferred_element_type=bf16)
    keep = ((lax.broadcasted_iota(jnp.int32, (rows, cols), 0) // SUB) % SUB
            == lax.broadcasted_iota(jnp.int32, (rows, cols), 1) % SUB)
    return jnp.where(keep, wide, 0).reshape(RADIX // JB, JB // SUB, 2 * SUB * RADIX, cols)


def _stage2_tables():
    shape = (RADIX * SUB, 2 * RADIX * SUB)
    r, c = (lax.broadcasted_iota(jnp.int32, shape, d) for d in range(2))
    k2, k1 = r // SUB, r % SUB
    part, j, k1_in = c // (RADIX * SUB), (c // SUB) % RADIX, c % SUB
    cos, sin = _trig(j * k2, RADIX)
    same = k1 == k1_in
    ha = jnp.where(same, jnp.where(part == 0, cos, -sin), 0.0)
    hp = jnp.where(same, jnp.where(part == 0, sin, cos), 0.0)
    return ha.astype(bf16), hp.astype(bf16)


def _encode(x, p):
    B, S, _ = x.shape
    assert S == RADIX * RADIX
    T = B * S
    x4 = x.reshape(B, RADIX, RADIX, D_MODEL)
    qkv, y1, res = _in_proj(x4, p["ln_in_g"], p["ln_in_b"], p["w_in"], p["dft1"])
    attn = _attention(qkv.reshape(B, S, QKV_COLS), p["bias"], p["sink"])
    x1 = _mix(y1.reshape(B, RADIX // SUB, 2, RADIX * SUB, FOUR_W), p["ha"], p["hp"], p["cw"],
              res, attn.reshape(B, RADIX, RADIX, ATTN_W),
              p["g_attn"], p["g_four"], p["w_o"], p["ln1_g"], p["ln1_b"])
    y = _ffn(x1.reshape(T, D_MODEL), S, p["w_up"], p["conv_w"], p["conv_b"], p["w_down"], p["ln2_g"], p["ln2_b"])
    return y.reshape(B, S, D_MODEL)


def kernel(x_prompt, x_sample, ln_in_g, ln_in_b, rel_table, w_in, attn_sink, w_fourier, g_attn,
           g_fourier, w_o, ln1_g, ln1_b, w_up, conv_w, conv_b, w_down, ln2_g, ln2_b):
    S = x_prompt.shape[1]
    row = lambda v: v.reshape(1, -1).astype(f32)
    col_scale = jnp.concatenate([jnp.full((ATTN_W,), HEAD_DIM ** -0.5 * LOG2E, f32),
                                 jnp.ones((IN_COLS - ATTN_W,), f32)])
    ha, hp = _stage2_tables()

    def by_pairs(w, axis):
        heads = jnp.split(w, N_HEADS, axis=axis)
        return jnp.concatenate([heads[h] for h in HEAD_ORDER], axis=axis)

    def by_chunks(w):
        lead = w.shape[:-1]
        return w.reshape(lead + (2, N_FF_CHUNKS, FF_CHUNK)).swapaxes(-3, -2).reshape(lead + (2 * D_FF,))

    gate_half = jnp.concatenate([jnp.ones((D_FF,), f32), jnp.full((D_FF,), 0.5, f32)])

    w_in_s = w_in[0] * col_scale
    w_in_p = jnp.concatenate([by_pairs(w_in_s[:, :ATTN_W], 1), w_in_s[:, ATTN_W:]], axis=1)
    w_o_p = jnp.concatenate([by_pairs(w_o[0][:ATTN_W], 0), w_o[0][ATTN_W:]], axis=0)

    p = dict(
        ln_in_g=row(ln_in_g), ln_in_b=row(ln_in_b),
        w_in=w_in_p.astype(bf16),
        bias=_band_bias(rel_table), sink=by_pairs(attn_sink[0].astype(f32), 0) * LOG2E,
        dft1=_stage1_tables(S), ha=ha, hp=hp,
        cw=_fold_channel_dft(w_fourier[0]),
        g_attn=row(by_pairs(g_attn[0], 0)), g_four=row(g_fourier[0]),
        w_o=w_o_p.astype(bf16),
        ln1_g=row(ln1_g[0]), ln1_b=row(ln1_b[0]),
        w_up=by_chunks(w_up[0]).astype(bf16),
        conv_w=by_chunks(conv_w[0].astype(f32) * gate_half),
        conv_b=by_chunks(row(conv_b[0]) * gate_half),
        w_down=w_down[0].astype(bf16),
        ln2_g=row(ln2_g[0]), ln2_b=row(ln2_b[0]),
    )
    return (_encode(x_prompt, p), _encode(x_sample, p))
```

```python
import functools
import math

import jax
import jax.numpy as jnp
from jax import lax
from jax.experimental import pallas as pl
from jax.experimental.pallas import tpu as pltpu

D_MODEL = 1024
HEAD_DIM = 64
N_HEADS = 8
N_KV_HEADS = 2
GROUP = N_HEADS // N_KV_HEADS
WINDOW = 128
BLOCK = 128
NUM_BUCKETS = 32
MAX_DISTANCE = 128
F_GROUPS = 4
F_CH = 128
ATTN_W = N_HEADS * HEAD_DIM
FOUR_W = F_GROUPS * F_CH
KV_COLS = N_KV_HEADS * HEAD_DIM
QKV_COLS = ATTN_W + 2 * KV_COLS
IN_COLS = QKV_COLS + FOUR_W
D_FF = 2816
EPS = 1e-5
DEPTH = 1
ALPHA = (2.0 * DEPTH) ** 0.25
NEG = -1e30
LOG2E = math.log2(math.e)
N_PAIRS = N_HEADS // 2
HEAD_ORDER = tuple(h for p in range(N_PAIRS) for h in (p, GROUP + p))
Q_BLOCKS = 16

ROW_TILE = 512
OUT_PARTS = 2
FF_CHUNK = 256
N_FF_CHUNKS = D_FF // FF_CHUNK
HALO = 8
RADIX = 64
JB = 16
SUB = 8
VMEM_LIMIT = 56 * 1024 * 1024
assert N_KV_HEADS == 2

bf16 = jnp.bfloat16
f32 = jnp.float32


def _layer_norm(x, g, b):
    mu = jnp.mean(x, axis=-1, keepdims=True)
    xc = x - mu
    var = jnp.mean(xc * xc, axis=-1, keepdims=True)
    return xc * lax.rsqrt(var + EPS) * g + b


def _rms_norm(x, g):
    return x * lax.rsqrt(jnp.mean(x * x, axis=-1, keepdims=True) + EPS) * g


def _dot(a, b):
    return jnp.dot(a, b, preferred_element_type=f32)


def _in_proj_kernel(x_ref, g_ref, b_ref, w_ref, dft_ref, qkv_ref, y_ref, res_ref):
    qkv = []
    for hh in range(JB // SUB):
        jj = slice(hh * SUB, (hh + 1) * SUB)
        x0 = _layer_norm(x_ref[0, :, jj, :].reshape(RADIX * SUB, D_MODEL), g_ref[...], b_ref[...])
        res_ref[0, :, jj, :] = (ALPHA * x0).reshape(RADIX, SUB, D_MODEL)
        h = _dot(x0.astype(bf16), w_ref[...])
        qkv.append(h[:, FOUR_W:].reshape(RADIX, SUB, QKV_COLS))
        y = _dot(dft_ref[0, hh], h[:, :FOUR_W].astype(bf16))
        y_ref[0, :, :, 0, hh * SUB * SUB:(hh + 1) * SUB * SUB, :] = (
            y.astype(bf16).reshape(RADIX // SUB, 2, SUB * SUB, FOUR_W))
    qkv_ref[0] = jnp.concatenate(qkv, axis=1).astype(bf16)


def _in_proj(x4, ln_g, ln_b, w_in, dft1):
    B = x4.shape[0]
    return pl.pallas_call(
        _in_proj_kernel,
        grid=(RADIX // JB, B),
        in_specs=[
            pl.BlockSpec((1, RADIX, JB, D_MODEL), lambda jb, b: (b, 0, jb, 0)),
            pl.BlockSpec((1, D_MODEL), lambda jb, b: (0, 0)),
            pl.BlockSpec((1, D_MODEL), lambda jb, b: (0, 0)),
            pl.BlockSpec((D_MODEL, IN_COLS), lambda jb, b: (0, 0)),
            pl.BlockSpec((1, JB // SUB, 2 * RADIX * SUB, RADIX * SUB), lambda jb, b: (jb, 0, 0, 0)),
        ],
        out_specs=[
            pl.BlockSpec((1, RADIX, JB, QKV_COLS), lambda jb, b: (b, 0, jb, 0)),
            pl.BlockSpec((1, RADIX // SUB, 2, 1, JB * SUB, FOUR_W), lambda jb, b: (b, 0, 0, jb, 0, 0)),
            pl.BlockSpec((1, RADIX, JB, D_MODEL), lambda jb, b: (b, 0, jb, 0)),
        ],
        out_shape=[
            jax.ShapeDtypeStruct((B, RADIX, RADIX, QKV_COLS), bf16),
            jax.ShapeDtypeStruct((B, RADIX // SUB, 2, RADIX // JB, JB * SUB, FOUR_W), bf16),
            jax.ShapeDtypeStruct((B, RADIX, RADIX, D_MODEL), f32),
        ],
        compiler_params=pltpu.CompilerParams(
            dimension_semantics=("arbitrary", "arbitrary"), vmem_limit_bytes=VMEM_LIMIT),
        name="in_proj",
    )(x4, ln_g, ln_b, w_in, dft1)


def _attn_kernel(q_ref, k_ref, v_ref, bias_ref, sink_ref, o_ref, *, nblk):
    t = pl.program_id(1)
    seq = nblk * BLOCK
    keys = 3 * BLOCK
    lane = lax.broadcasted_iota(jnp.int32, (keys, KV_COLS), 1)
    low = lane < HEAD_DIM
    out_low = lax.broadcasted_iota(jnp.int32, (BLOCK, KV_COLS), 1) < HEAD_DIM

    def block_diag(ref, starts):
        win = jnp.concatenate([ref[0, pl.ds(s, BLOCK), :] for s in starts], axis=0)
        zero = jnp.zeros_like(win)
        return jnp.concatenate([jnp.where(low, win, zero), jnp.where(low, zero, win)], axis=0)

    def scores(qi):
        i = t * Q_BLOCKS + qi
        starts = [pl.multiple_of(jnp.clip((i + c) * BLOCK, 0, seq - BLOCK), BLOCK) for c in (-1, 0, 1)]
        q = q_ref[0, qi * BLOCK:(qi + 1) * BLOCK, :]
        qs = jnp.concatenate([q[:, p * KV_COLS:(p + 1) * KV_COLS] for p in range(N_PAIRS)], axis=0)
        logits = lax.dot_general(qs, block_diag(k_ref, starts), (((1,), (1,)), ((), ())),
                                 preferred_element_type=f32)
        variant = jnp.where(i == 0, 1, jnp.where(i == nblk - 1, 2, 0))
        return logits + bias_ref[variant], block_diag(v_ref, starts)

    def finish(qi, logits, vbd):
        probs, scales = [], []
        for p in range(N_PAIRS):
            halves, recips = [], []
            for side in range(2):
                l = logits[p * BLOCK:(p + 1) * BLOCK, side * keys:(side + 1) * keys]
                sink = sink_ref[2 * p + side]
                m = jnp.maximum(jnp.max(l, axis=-1, keepdims=True), sink)
                e = jnp.exp2(l - m)
                denom = jnp.sum(e, axis=-1, keepdims=True) + jnp.exp2(sink - m)
                halves.append(e.astype(bf16))
                recips.append(1.0 / denom)
            probs.append(jnp.concatenate(halves, axis=-1))
            scales.append(jnp.where(out_low, recips[0], recips[1]))
        pv = _dot(jnp.concatenate(probs, axis=0), vbd)
        out = [pv[p * BLOCK:(p + 1) * BLOCK, :] * scales[p] for p in range(N_PAIRS)]
        o_ref[0, qi * BLOCK:(qi + 1) * BLOCK, :] = jnp.concatenate(out, axis=-1).astype(bf16)

    staged = scores(0)
    for qi in range(Q_BLOCKS):
        ahead = scores(qi + 1) if qi + 1 < Q_BLOCKS else None
        finish(qi, *staged)
        staged = ahead


def _attention(qkv, bias, sink):
    B, S, _ = qkv.shape
    nblk = S // BLOCK
    assert nblk >= 2 and nblk % Q_BLOCKS == 0
    k_col = ATTN_W // KV_COLS
    rows = Q_BLOCKS * BLOCK
    return pl.pallas_call(
        functools.partial(_attn_kernel, nblk=nblk),
        grid=(B, nblk // Q_BLOCKS),
        in_specs=[
            pl.BlockSpec((1, rows, ATTN_W), lambda b, t: (b, t, 0)),
            pl.BlockSpec((1, S, KV_COLS), lambda b, t: (b, 0, k_col)),
            pl.BlockSpec((1, S, KV_COLS), lambda b, t: (b, 0, k_col + 1)),
            pl.BlockSpec((3, N_PAIRS * BLOCK, 6 * BLOCK), lambda b, t: (0, 0, 0)),
            pl.BlockSpec(memory_space=pltpu.SMEM),
        ],
        out_specs=pl.BlockSpec((1, rows, ATTN_W), lambda b, t: (b, t, 0)),
        out_shape=jax.ShapeDtypeStruct((B, S, ATTN_W), bf16),
        compiler_params=pltpu.CompilerParams(
            dimension_semantics=("arbitrary", "arbitrary"), vmem_limit_bytes=VMEM_LIMIT),
        name="attn",
    )(qkv, qkv, qkv, bias, sink)


def _bias_kernel(bucket_ref, rel_ref, table_ref, o_ref):
    bucket = bucket_ref[...]
    in_window = jnp.abs(rel_ref[...]) <= WINDOW
    for h in range(N_HEADS):
        acc = jnp.zeros(bucket.shape, f32)
        for b in range(NUM_BUCKETS):
            acc = jnp.where(bucket == b, table_ref[b, h], acc)
        o_ref[h] = jnp.where(in_window, acc * LOG2E, NEG)


def _band_bias(rel_table):
    rel = (jnp.arange(3 * BLOCK, dtype=jnp.int32)[None, :] - BLOCK) - jnp.arange(BLOCK, dtype=jnp.int32)[:, None]
    bias = pl.pallas_call(
        _bias_kernel,
        in_specs=[pl.BlockSpec(rel.shape, lambda: (0, 0)), pl.BlockSpec(rel.shape, lambda: (0, 0)),
                  pl.BlockSpec(memory_space=pltpu.SMEM)],
        out_specs=pl.BlockSpec((N_HEADS,) + rel.shape, lambda: (0, 0, 0)),
        out_shape=jax.ShapeDtypeStruct((N_HEADS,) + rel.shape, f32),
        name="bias_table",
    )(_t5_bucket(rel), rel, rel_table.astype(f32))
    chunk = jnp.arange(3 * BLOCK, dtype=jnp.int32) // BLOCK
    variants = jnp.stack([bias, jnp.where(chunk == 0, NEG, bias), jnp.where(chunk == 2, NEG, bias)])
    pairs = jnp.stack([variants[:, h] for h in HEAD_ORDER], axis=1)
    pairs = pairs.reshape(3, N_PAIRS, 2, BLOCK, 3 * BLOCK)
    return pairs.transpose(0, 1, 3, 2, 4).reshape(3, N_PAIRS * BLOCK, 6 * BLOCK)


def _fourier_rows(ys, ha_ref, hp_ref, cw_ref):
    a = _dot(ha_ref[...], ys).astype(bf16)
    p = _dot(hp_ref[...], ys).astype(bf16)
    four = []
    for g in range(F_GROUPS):
        cols = slice(g * F_CH, (g + 1) * F_CH)
        ap = jnp.concatenate([a[:, cols], p[:, cols]], axis=-1)
        four.append(_dot(ap, cw_ref[g]))
    return jnp.concatenate(four, axis=-1)


def _fold_kernel(cs_ref, wf_ref, o_ref):
    for g in range(F_GROUPS):
        o_ref[g] = jnp.dot(cs_ref[...], wf_ref[g], preferred_element_type=f32,
                           precision=lax.Precision.HIGHEST).astype(bf16)


def _fold_channel_dft(w_f):
    ar = jnp.arange(F_CH, dtype=jnp.int32)
    cc, sc = _trig(ar[:, None] * ar[None, :], F_CH)
    cs = jnp.concatenate([cc, -sc], axis=0) * F_CH ** -0.5
    return pl.pallas_call(
        _fold_kernel,
        out_shape=jax.ShapeDtypeStruct((F_GROUPS, 2 * F_CH, F_CH), bf16),
        name="fold_channel_dft",
    )(cs, w_f.astype(f32))


def _mix_kernel(y_ref, ha_ref, hp_ref, cw_ref, res_ref, attn_ref, ga_ref, gf_ref,
                wo_ref, l1g_ref, l1b_ref, x1_ref):
    rows = RADIX * SUB
    attn = attn_ref[0].astype(f32)
    for hh in range(JB // SUB):
        kk = slice(hh * SUB, (hh + 1) * SUB)
        ys = y_ref[0, hh].reshape(2 * rows, FOUR_W)
        fn = _rms_norm(_fourier_rows(ys, ha_ref, hp_ref, cw_ref), gf_ref[...]).astype(bf16)
        an = _rms_norm(attn[:, kk, :].reshape(rows, ATTN_W), ga_ref[...]).astype(bf16)
        for k2 in (slice(0, RADIX // 2), slice(RADIX // 2, RADIX)):
            r = slice(k2.start * SUB, k2.stop * SUB)
            mix = _dot(an[r], wo_ref[:ATTN_W, :]) + _dot(fn[r], wo_ref[ATTN_W:, :])
            res = res_ref[0, k2, kk, :].reshape(rows // 2, D_MODEL)
            x1 = _layer_norm(res + mix, l1g_ref[...], l1b_ref[...])
            x1_ref[0, k2, kk, :] = x1.reshape(RADIX // 2, SUB, D_MODEL)


def _mix(y, ha, hp, cw, res, attn, g_attn, g_four, w_o, ln1_g, ln1_b):
    B = y.shape[0]
    groups = JB // SUB
    full = lambda *shape: pl.BlockSpec(shape, lambda b, kb: (0,) * len(shape))
    tile = lambda w: pl.BlockSpec((1, RADIX, JB, w), lambda b, kb: (b, 0, kb, 0))
    return pl.pallas_call(
        _mix_kernel,
        grid=(B, RADIX // JB),
        in_specs=[
            pl.BlockSpec((1, groups, 2, RADIX * SUB, FOUR_W), lambda b, kb: (b, kb, 0, 0, 0)),
            full(RADIX * SUB, 2 * RADIX * SUB), full(RADIX * SUB, 2 * RADIX * SUB),
            full(F_GROUPS, 2 * F_CH, F_CH),
            tile(D_MODEL), tile(ATTN_W),
            full(1, ATTN_W), full(1, FOUR_W), full(D_MODEL, D_MODEL),
            full(1, D_MODEL), full(1, D_MODEL),
        ],
        out_specs=tile(D_MODEL),
        out_shape=jax.ShapeDtypeStruct((B, RADIX, RADIX, D_MODEL), f32),
        compiler_params=pltpu.CompilerParams(
            dimension_semantics=("arbitrary", "arbitrary"), vmem_limit_bytes=VMEM_LIMIT),
        name="mix",
    )(y, ha, hp, cw, res, attn, g_attn, g_four, w_o, ln1_g, ln1_b)


def _gelu_scaled(z):
    return z * (1.0 + lax.erf(z))


def _ffn_kernel(xp_ref, xm_ref, xn_ref, wup_ref, cw_ref, cb_ref, wd_ref, l2g_ref, l2b_ref,
                y_ref, xs_ref, h_ref, act_ref, *, tiles_per_seq):
    t = pl.program_id(0)
    first = (t % tiles_per_seq) == 0
    last = (t % tiles_per_seq) == tiles_per_seq - 1
    xs_ref[...] = jnp.concatenate([jnp.where(first, 0.0, xp_ref[...]), xm_ref[...],
                                   jnp.where(last, 0.0, xn_ref[...])], axis=0).astype(bf16)

    def conv(c):
        cols = slice(2 * c * FF_CHUNK, 2 * (c + 1) * FF_CHUNK)
        cw = cw_ref[:, cols]
        wide = h_ref[c % 2]
        n = ROW_TILE + 2 * HALO
        prev = pltpu.roll(wide, 1, 0)[HALO:HALO + ROW_TILE]
        nxt = pltpu.roll(wide, n - 1, 0)[HALO:HALO + ROW_TILE]
        return (cw[0:1, :] * prev + cw[1:2, :] * wide[HALO:HALO + ROW_TILE] + cw[2:3, :] * nxt
                + cb_ref[:, cols])

    def up(c):
        w = jnp.concatenate([wup_ref[:, c * FF_CHUNK:(c + 1) * FF_CHUNK],
                             wup_ref[:, D_FF + c * FF_CHUNK:D_FF + (c + 1) * FF_CHUNK]], axis=1)
        h_ref[c % 2] = _dot(xs_ref[...], w)

    up(0)
    for c in range(N_FF_CHUNKS):
        if c + 1 < N_FF_CHUNKS:
            up(c + 1)
        au = conv(c)
        act = _gelu_scaled(au[:, :FF_CHUNK]) * au[:, FF_CHUNK:]
        act_ref[:, c * FF_CHUNK:(c + 1) * FF_CHUNK] = act.astype(bf16)
    part = ROW_TILE // OUT_PARTS
    for rows in (slice(r * part, (r + 1) * part) for r in range(OUT_PARTS)):
        ffn = _dot(act_ref[rows, :], wd_ref[...])
        y_ref[rows, :] = _layer_norm(ALPHA * xm_ref[rows, :] + ffn, l2g_ref[...], l2b_ref[...])


def _ffn(x1, seq_len, w_up, conv_w, conv_b, w_down, ln2_g, ln2_b):
    T = x1.shape[0]
    tiles_per_seq = seq_len // ROW_TILE
    hb = ROW_TILE // HALO
    n_hb = T // HALO
    full = lambda *shape: pl.BlockSpec(shape, lambda t: (0,) * len(shape),
                                       pipeline_mode=pl.Buffered(1))
    m = ROW_TILE + 2 * HALO
    return pl.pallas_call(
        functools.partial(_ffn_kernel, tiles_per_seq=tiles_per_seq),
        grid=(T // ROW_TILE,),
        in_specs=[
            pl.BlockSpec((HALO, D_MODEL), lambda t: (jnp.maximum(t * hb - 1, 0), 0)),
            pl.BlockSpec((ROW_TILE, D_MODEL), lambda t: (t, 0)),
            pl.BlockSpec((HALO, D_MODEL), lambda t: (jnp.minimum((t + 1) * hb, n_hb - 1), 0)),
            full(D_MODEL, 2 * D_FF), full(3, 2 * D_FF), full(1, 2 * D_FF),
            full(D_FF, D_MODEL), full(1, D_MODEL), full(1, D_MODEL),
        ],
        out_specs=pl.BlockSpec((ROW_TILE, D_MODEL), lambda t: (t, 0)),
        out_shape=jax.ShapeDtypeStruct((T, D_MODEL), f32),
        scratch_shapes=[
            pltpu.VMEM((m, D_MODEL), bf16),
            pltpu.VMEM((2, m, 2 * FF_CHUNK), f32),
            pltpu.VMEM((ROW_TILE, D_FF), bf16),
        ],
        compiler_params=pltpu.CompilerParams(
            dimension_semantics=("arbitrary",), vmem_limit_bytes=VMEM_LIMIT),
        name="ffn",
    )(x1, x1, x1, w_up, conv_w, conv_b, w_down, ln2_g, ln2_b)


def _t5_bucket(rel):
    nb = NUM_BUCKETS // 2
    ret = jnp.where(rel > 0, nb, 0)
    n = jnp.abs(rel)
    max_exact = nb // 2
    nf = jnp.maximum(n, 1).astype(f32)
    large = max_exact + (jnp.log(nf / max_exact) / math.log(MAX_DISTANCE / max_exact)
                         * (nb - max_exact)).astype(jnp.int32)
    large = jnp.minimum(large, nb - 1)
    return ret + jnp.where(n < max_exact, n, large)


def _trig(num, den):
    ang = (num % den).astype(f32) * (2.0 * math.pi / den)
    return jnp.cos(ang), jnp.sin(ang)


def _stage1_tables(seq):
    shape = (RADIX // JB, JB // SUB, RADIX // SUB, 2, SUB, SUB, RADIX)
    jb, hh, kb, part, jj, k1, i = (lax.broadcasted_iota(jnp.int32, shape, d) for d in range(7))
    phase = (kb * SUB + k1) * (RADIX * i + jb * JB + hh * SUB + jj) - part * (seq // 4)
    dense = (_trig(phase, seq)[0] * (seq ** -0.5)).astype(bf16).reshape(-1, RADIX)
    rows, cols = dense.shape[0], RADIX * SUB
    spread = (lax.broadcasted_iota(jnp.int32, (RADIX, cols), 1) // SUB
              == lax.broadcasted_iota(jnp.int32, (RADIX, cols), 0)).astype(bf16)
    wide = jnp.dot(dense, spread, preferred_element_type=bf16)
    keep = ((lax.broadcasted_iota(jnp.int32, (rows, cols), 0) // SUB) % SUB
            == lax.broadcasted_iota(jnp.int32, (rows, cols), 1) % SUB)
    return jnp.where(keep, wide, 0).reshape(RADIX // JB, JB // SUB, 2 * SUB * RADIX, cols)


def _stage2_tables():
    shape = (RADIX * SUB, 2 * RADIX * SUB)
    r, c = (lax.broadcasted_iota(jnp.int32, shape, d) for d in range(2))
    k2, k1 = r // SUB, r % SUB
    part, j, k1_in = c // (RADIX * SUB), (c // SUB) % RADIX, c % SUB
    cos, sin = _trig(j * k2, RADIX)
    same = k1 == k1_in
    ha = jnp.where(same, jnp.where(part == 0, cos, -sin), 0.0)
    hp = jnp.where(same, jnp.where(part == 0, sin, cos), 0.0)
    return ha.astype(bf16), hp.astype(bf16)


def _encode(x, p):
    B, S, _ = x.shape
    assert S == RADIX * RADIX
    T = B * S
    x4 = x.reshape(B, RADIX, RADIX, D_MODEL)
    qkv, y1, res = _in_proj(x4, p["ln_in_g"], p["ln_in_b"], p["w_in"], p["dft1"])
    attn = _attention(qkv.reshape(B, S, QKV_COLS), p["bias"], p["sink"])
    x1 = _mix(y1.reshape(B, RADIX // SUB, 2, RADIX * SUB, FOUR_W), p["ha"], p["hp"], p["cw"],
              res, attn.reshape(B, RADIX, RADIX, ATTN_W),
              p["g_attn"], p["g_four"], p["w_o"], p["ln1_g"], p["ln1_b"])
    y = _ffn(x1.reshape(T, D_MODEL), S, p["w_up"], p["conv_w"], p["conv_b"], p["w_down"], p["ln2_g"], p["ln2_b"])
    return y.reshape(B, S, D_MODEL)


def kernel(x_prompt, x_sample, ln_in_g, ln_in_b, rel_table, w_in, attn_sink, w_fourier, g_attn,
           g_fourier, w_o, ln1_g, ln1_b, w_up, conv_w, conv_b, w_down, ln2_g, ln2_b):
    S = x_prompt.shape[1]
    row = lambda v: v.reshape(1, -1).astype(f32)
    col_scale = jnp.concatenate([jnp.full((ATTN_W,), HEAD_DIM ** -0.5 * LOG2E, f32),
                                 jnp.ones((IN_COLS - ATTN_W,), f32)])
    ha, hp = _stage2_tables()

    def by_pairs(w, axis):
        heads = jnp.split(w, N_HEADS, axis=axis)
        return jnp.concatenate([heads[h] for h in HEAD_ORDER], axis=axis)

    def by_chunks(w):
        lead = w.shape[:-1]
        return w.reshape(lead + (2, N_FF_CHUNKS, FF_CHUNK)).swapaxes(-3, -2).reshape(lead + (2 * D_FF,))

    conv_scale = 2.0 ** -0.5

    w_in_s = w_in[0] * col_scale
    w_in_p = jnp.concatenate([w_in_s[:, QKV_COLS:], by_pairs(w_in_s[:, :ATTN_W], 1),
                              w_in_s[:, ATTN_W:QKV_COLS]], axis=1)
    w_o_p = jnp.concatenate([by_pairs(w_o[0][:ATTN_W], 0), w_o[0][ATTN_W:]], axis=0)

    p = dict(
        ln_in_g=row(ln_in_g), ln_in_b=row(ln_in_b),
        w_in=w_in_p.astype(bf16),
        bias=_band_bias(rel_table), sink=by_pairs(attn_sink[0].astype(f32), 0) * LOG2E,
        dft1=_stage1_tables(S), ha=ha, hp=hp,
        cw=_fold_channel_dft(w_fourier[0]),
        g_attn=row(by_pairs(g_attn[0], 0)), g_four=row(g_fourier[0]),
        w_o=w_o_p.astype(bf16),
        ln1_g=row(ln1_g[0]), ln1_b=row(ln1_b[0]),
        w_up=w_up[0].astype(bf16),
        conv_w=by_chunks(conv_w[0].astype(f32) * conv_scale),
        conv_b=by_chunks(row(conv_b[0]) * conv_scale),
        w_down=w_down[0].astype(bf16),
        ln2_g=row(ln2_g[0]), ln2_b=row(ln2_b[0]),
    )
    return (_encode(x_prompt, p), _encode(x_sample, p))
```
